```python
import math
import jax, jax.numpy as jnp
from jax import lax
import numpy as np

D_MODEL = 1024
BATCH = 4
SEQ = 4096
DEPTH = 1

MIX_WIDTH = D_MODEL
ATTN_WIDTH = D_MODEL // 2
CONV_WIDTH = MIX_WIDTH - ATTN_WIDTH
HEAD_DIM = 64
N_HEADS = ATTN_WIDTH // HEAD_DIM
ROT_DIM = HEAD_DIM // 4
ROPE_THETA = 500000.0
MOBA_BLOCK = 256
MOBA_TOP_K = 3
Q_CHUNK = 64
CONV_WIDTH_K = 31
D_FF = 4 * D_MODEL
EPS = 1e-6
IN_COLS = 3 * ATTN_WIDTH + 2 * CONV_WIDTH

kernel_name = "hybrid_moba_conformer_parallel_heads"


def rms_norm(x, g):
    xf = x.astype(jnp.float32)
    y = xf * lax.rsqrt(jnp.mean(xf * xf, axis=-1, keepdims=True) + EPS)
    return (y * g.astype(jnp.float32)).astype(x.dtype)


def layer_norm(x, g, b):
    xf = x.astype(jnp.float32)
    mu = jnp.mean(xf, axis=-1, keepdims=True)
    var = jnp.mean(jnp.square(xf - mu), axis=-1, keepdims=True)
    y = (xf - mu) * lax.rsqrt(var + EPS)
    return (y * g.astype(jnp.float32) + b.astype(jnp.float32)).astype(x.dtype)


def rope_tables(seq, dtype):
    half = ROT_DIM // 2
    inv_freq = ROPE_THETA ** (-jnp.arange(half, dtype=jnp.float32) * 2.0 / ROT_DIM)
    ang = jnp.arange(seq, dtype=jnp.float32)[:, None] * inv_freq[None, :]
    return jnp.cos(ang).astype(dtype), jnp.sin(ang).astype(dtype)


def partial_rope(x, cos, sin):
    half = ROT_DIM // 2
    x1 = x[..., :half]
    x2 = x[..., half:ROT_DIM]
    rot = jnp.concatenate([x1 * cos - x2 * sin, x2 * cos + x1 * sin], axis=-1)
    return jnp.concatenate([rot, x[..., ROT_DIM:]], axis=-1)


def moba_attention(q, k, v):
    B, H, S, Dh = q.shape
    s_pad = -(-S // MOBA_BLOCK) * MOBA_BLOCK
    padw = ((0, 0), (0, 0), (0, s_pad - S), (0, 0))
    q = jnp.pad(q, padw)
    k = jnp.pad(k, padw)
    v = jnp.pad(v, padw)
    nb = s_pad // MOBA_BLOCK
    k_blk = k.reshape(B, H, nb, MOBA_BLOCK, Dh)
    v_blk = v.reshape(B, H, nb, MOBA_BLOCK, Dh)
    k_mean = jnp.mean(k_blk.astype(jnp.float32), axis=3)
    gate = jnp.einsum('bhsd,bhnd->bhsn', q.astype(jnp.float32), k_mean)
    q_blk = jnp.arange(s_pad) // MOBA_BLOCK
    past = jnp.arange(nb)[None, :] < q_blk[:, None]
    gate = jnp.where(past[None, None], gate, -jnp.inf)
    n_sel = min(MOBA_TOP_K, max(nb - 1, 1))
    _, sel_idx = lax.top_k(gate, n_sel)
    sel_ok = sel_idx < q_blk[None, None, :, None]

    n_chunks = s_pad // Q_CHUNK
    q_c = jnp.moveaxis(q.reshape(B, H, n_chunks, Q_CHUNK, Dh), 2, 0)
    idx_c = jnp.moveaxis(sel_idx.reshape(B, H, n_chunks, Q_CHUNK, n_sel), 2, 0)
    ok_c = jnp.moveaxis(sel_ok.reshape(B, H, n_chunks, Q_CHUNK, n_sel), 2, 0)
    gather = jax.vmap(jax.vmap(lambda blocks, ix: blocks[ix]))
    scale = HEAD_DIM ** -0.5

    def step(args):
        c, qb, ix, ok = args
        start = c * Q_CHUNK
        own = start // MOBA_BLOCK
        k_own = lax.dynamic_index_in_dim(k_blk, own, axis=2, keepdims=False)
        v_own = lax.dynamic_index_in_dim(v_blk, own, axis=2, keepdims=False)
        qpos = start + jnp.arange(Q_CHUNK)
        kpos = own * MOBA_BLOCK + jnp.arange(MOBA_BLOCK)
        own_logits = jnp.einsum('bhqd,bhkd->bhqk', qb, k_own).astype(jnp.float32) * scale
        own_logits = jnp.where((kpos[None, :] <= qpos[:, None])[None, None], own_logits, -jnp.inf)
        k_sel = gather(k_blk, ix)
        v_sel = gather(v_blk, ix)
        sel_logits = jnp.einsum('bhqd,bhqjtd->bhqjt', qb, k_sel).astype(jnp.float32) * scale
        sel_logits = jnp.where(ok[..., None], sel_logits, -jnp.inf)
        sel_logits = sel_logits.reshape(B, H, Q_CHUNK, n_sel * MOBA_BLOCK)
        probs = jax.nn.softmax(jnp.concatenate([sel_logits, own_logits], axis=-1), axis=-1)
        probs = probs.astype(v.dtype)
        p_sel = probs[..., :n_sel * MOBA_BLOCK].reshape(B, H, Q_CHUNK, n_sel, MOBA_BLOCK)
        p_own = probs[..., n_sel * MOBA_BLOCK:]
        return (jnp.einsum('bhqjt,bhqjtd->bhqd', p_sel, v_sel)
                + jnp.einsum('bhqk,bhkd->bhqd', p_own, v_own))

    out = lax.map(step, (jnp.arange(n_chunks), q_c, idx_c, ok_c))
    out = jnp.moveaxis(out, 0, 2).reshape(B, H, s_pad, Dh)
    return out[:, :, :S]


def conformer_conv(u, b_glu, w_dw, b_dw, g_ln, b_ln):
    u = u + b_glu
    a, gte = jnp.split(u, 2, axis=-1)
    h = a * jax.nn.sigmoid(gte)
    h = lax.conv_general_dilated(h, w_dw, window_strides=(1,),
                                 padding=[(CONV_WIDTH_K - 1, 0)],
                                 dimension_numbers=('NWC', 'WIO', 'NWC'),
                                 feature_group_count=CONV_WIDTH) + b_dw
    h = layer_norm(h, g_ln, b_ln)
    return jax.nn.swish(h)


def setup_inputs(seed: int = 0) -> dict:
    key = jax.random.key(seed)
    ks = jax.random.split(key, 16)
    f32 = jnp.float32
    L = DEPTH
    nrm = lambda k, shape, fan: jax.random.normal(k, shape, f32) * fan ** -0.5
    return {
        "x": jax.random.normal(ks[0], (BATCH, SEQ, D_MODEL), f32),
        "g_mix_norm": 1.0 + 0.02 * jax.random.normal(ks[1], (L, D_MODEL), f32),
        "w_in": nrm(ks[2], (L, D_MODEL, IN_COLS), D_MODEL),
        "b_glu": 0.02 * jax.random.normal(ks[3], (L, 2 * CONV_WIDTH), f32),
        "w_dw": nrm(ks[4], (L, CONV_WIDTH_K, 1, CONV_WIDTH), CONV_WIDTH_K),
        "b_dw": 0.02 * jax.random.normal(ks[5], (L, CONV_WIDTH), f32),
        "g_conv_ln": 1.0 + 0.02 * jax.random.normal(ks[6], (L, CONV_WIDTH), f32),
        "b_conv_ln": 0.02 * jax.random.normal(ks[7], (L, CONV_WIDTH), f32),
        "w_out": nrm(ks[8], (L, MIX_WIDTH, D_MODEL), MIX_WIDTH),
        "g_mlp_norm": 1.0 + 0.02 * jax.random.normal(ks[9], (L, D_MODEL), f32),
        "w_mlp_in": nrm(ks[10], (L, D_MODEL, D_FF), D_MODEL),
        "w_mlp_out": nrm(ks[11], (L, D_FF, D_MODEL), D_FF),
        "g_final": 1.0 + 0.02 * jax.random.normal(ks[12], (D_MODEL,), f32),
    }


def reference(x, g_mix_norm, w_in, b_glu, w_dw, b_dw, g_conv_ln, b_conv_ln,
              w_out, g_mlp_norm, w_mlp_in, w_mlp_out, g_final):
    B, S, _ = x.shape
    cos, sin = rope_tables(S, x.dtype)
    h = x
    for l in range(DEPTH):
        xn = rms_norm(h, g_mix_norm[l])
        proj = xn @ w_in[l]
        q, k, v, u = jnp.split(proj, [ATTN_WIDTH, 2 * ATTN_WIDTH, 3 * ATTN_WIDTH], axis=-1)
        to_heads = lambda t: t.reshape(B, S, N_HEADS, HEAD_DIM).transpose(0, 2, 1, 3)
        q = partial_rope(to_heads(q), cos, sin)
        k = partial_rope(to_heads(k), cos, sin)
        attn = moba_attention(q, k, to_heads(v))
        attn = attn.transpose(0, 2, 1, 3).reshape(B, S, ATTN_WIDTH)
        conv = conformer_conv(u, b_glu[l], w_dw[l], b_dw[l], g_conv_ln[l], b_conv_ln[l])
        mixed = jnp.concatenate([attn, conv], axis=-1)
        h = h + mixed @ w_out[l]
        hn = rms_norm(h, g_mlp_norm[l])
        ff = jnp.square(jax.nn.relu(hn @ w_mlp_in[l]))
        h = h + ff @ w_mlp_out[l]
    return rms_norm(h, g_final)
```

```python
import functools

import jax
import jax.numpy as jnp
from jax import lax
from jax.experimental import pallas as pl
from jax.experimental.pallas import tpu as pltpu

D_MODEL = 1024
ATTN_WIDTH = 512
CONV_WIDTH = 512
HEAD_DIM = 64
N_HEADS = 8
ROT_DIM = 16
ROT_HALF = 8
ROPE_THETA = 500000.0
MOBA_BLOCK = 256
MOBA_TOP_K = 3
CONV_K = 31
D_FF = 4096
EPS = 1e-6

Q_ROWS = 128
K_LANES = 128
BIAS_ROW0 = HEAD_DIM
V_ROWS = 80
MASKED = -1e30

VMEM_LIMIT = 56 * 1024 * 1024

_NT = (((1,), (1,)), ((), ()))


def _rms(x, g):
    return x * lax.rsqrt(jnp.mean(x * x, axis=-1, keepdims=True) + EPS) * g


def _sigmoid(x):
    return 1.0 / (1.0 + jnp.exp(-x))


def _in_proj_kernel(x_ref, g_ref, wq_ref, wk_ref, wv_ref, wu_ref, bglu_ref,
                    cosT_ref, sinT_ref, c_ref, sa_ref, sb_ref,
                    qT_ref, k_ref, vT_ref, kmean_ref, hglu_ref):
    n = pl.program_id(1)
    xb = _rms(x_ref[0], g_ref[...]).astype(jnp.bfloat16)

    qT = lax.dot_general(wq_ref[...], xb, _NT, preferred_element_type=jnp.float32)
    q3 = qT.reshape(N_HEADS, Q_ROWS, MOBA_BLOCK)
    cos = cosT_ref[...][None]
    sin = sinT_ref[...][None]
    x1 = q3[:, 0:ROT_HALF]
    x2 = q3[:, ROT_HALF:ROT_DIM]
    q3 = jnp.concatenate([x1 * cos - x2 * sin, x2 * cos + x1 * sin, q3[:, ROT_DIM:]], axis=1)
    qT_ref[0] = q3.reshape(N_HEADS * Q_ROWS, MOBA_BLOCK).astype(jnp.bfloat16)

    k = jnp.dot(xb, wk_ref[...], preferred_element_type=jnp.float32)
    c = c_ref[...]
    sa = sa_ref[...]
    sb = sb_ref[...]
    parts = []
    for h in range(N_HEADS):
        kh = k[:, h * K_LANES:(h + 1) * K_LANES]
        parts.append(kh * c + pltpu.roll(kh, K_LANES - ROT_HALF, 1) * sa + pltpu.roll(kh, ROT_HALF, 1) * sb)
    k = jnp.concatenate(parts, axis=1)
    kmean_ref[0, pl.ds(n, 1), :] = jnp.mean(k, axis=0, keepdims=True)
    lane = lax.broadcasted_iota(jnp.int32, k.shape, 1) % K_LANES
    k_ref[0, 0] = jnp.where(lane == BIAS_ROW0 + n, 1.0, k).astype(jnp.bfloat16)

    vT = lax.dot_general(wv_ref[...], xb, _NT, preferred_element_type=jnp.float32)
    row = lax.broadcasted_iota(jnp.int32, vT.shape, 0) % V_ROWS
    vT_ref[0, 0] = jnp.where(row == HEAD_DIM, 1.0, vT).astype(jnp.bfloat16)

    u = jnp.dot(xb, wu_ref[...], preferred_element_type=jnp.float32) + bglu_ref[...]
    hglu_ref[0] = u[:, :CONV_WIDTH] * _sigmoid(u[:, CONV_WIDTH:])


def _in_proj(x, g, wq, wk, wv, wu, bglu, cosT, sinT, c, sa, sb):
    B, S, _ = x.shape
    nb = S // MOBA_BLOCK
    const = lambda shape: pl.BlockSpec(shape, lambda b, n: (0,) * len(shape))
    return pl.pallas_call(
        _in_proj_kernel,
        grid=(B, nb),
        in_specs=[
            pl.BlockSpec((1, MOBA_BLOCK, D_MODEL), lambda b, n: (b, n, 0)),
            const((1, D_MODEL)),
            const(wq.shape), const(wk.shape), const(wv.shape), const(wu.shape),
            const((1, 2 * CONV_WIDTH)),
            pl.BlockSpec((ROT_HALF, MOBA_BLOCK), lambda b, n: (0, n)),
            pl.BlockSpec((ROT_HALF, MOBA_BLOCK), lambda b, n: (0, n)),
            pl.BlockSpec((MOBA_BLOCK, K_LANES), lambda b, n: (n, 0)),
            pl.BlockSpec((MOBA_BLOCK, K_LANES), lambda b, n: (n, 0)),
            pl.BlockSpec((MOBA_BLOCK, K_LANES), lambda b, n: (n, 0)),
        ],
        out_specs=[
            pl.BlockSpec((1, N_HEADS * Q_ROWS, MOBA_BLOCK), lambda b, n: (b, 0, n)),
            pl.BlockSpec((1, 1, MOBA_BLOCK, N_HEADS * K_LANES), lambda b, n: (b, n, 0, 0)),
            pl.BlockSpec((1, 1, N_HEADS * V_ROWS, MOBA_BLOCK), lambda b, n: (b, n, 0, 0)),
            pl.BlockSpec((1, nb, N_HEADS * K_LANES), lambda b, n: (b, 0, 0)),
            pl.BlockSpec((1, MOBA_BLOCK, CONV_WIDTH), lambda b, n: (b, n, 0)),
        ],
        out_shape=[
            jax.ShapeDtypeStruct((B, N_HEADS * Q_ROWS, S), jnp.bfloat16),
            jax.ShapeDtypeStruct((B, nb, MOBA_BLOCK, N_HEADS * K_LANES), jnp.bfloat16),
            jax.ShapeDtypeStruct((B, nb, N_HEADS * V_ROWS, MOBA_BLOCK), jnp.bfloat16),
            jax.ShapeDtypeStruct((B, nb, N_HEADS * K_LANES), jnp.float32),
            jax.ShapeDtypeStruct((B, S, CONV_WIDTH), jnp.float32),
        ],
        compiler_params=pltpu.CompilerParams(
            dimension_semantics=("parallel", "arbitrary"), vmem_limit_bytes=VMEM_LIMIT),
        name="in_proj",
    )(x, g, wq, wk, wv, wu, bglu, cosT, sinT, c, sa, sb)


def _moba_kernel(qT_ref, k_ref, vT_ref, kmean_ref, o_ref):
    i = pl.program_id(2)
    nb = k_ref.shape[1]
    qT = qT_ref[0]

    km = kmean_ref[0]
    km_hi = km.astype(jnp.bfloat16)
    km_lo = (km - km_hi.astype(jnp.float32)).astype(jnp.bfloat16)
    g2 = jnp.dot(jnp.concatenate([km_hi, km_lo], axis=0), qT, preferred_element_type=jnp.float32)
    gate = g2[:nb] + g2[nb:]

    blk = lax.broadcasted_iota(jnp.int32, gate.shape, 0)
    rem = jnp.where(blk < i, gate, -jnp.inf)
    sel = blk == i
    for _ in range(MOBA_TOP_K):
        mx = jnp.max(rem, axis=0, keepdims=True)
        first = jnp.min(jnp.where(rem == mx, blk, nb), axis=0, keepdims=True)
        pick = (blk == first) & (mx > -jnp.inf)
        sel = sel | pick
        rem = jnp.where(pick, -jnp.inf, rem)
    bias = jnp.where(sel, 0.0, MASKED).astype(jnp.bfloat16)

    pad = jnp.zeros((Q_ROWS - HEAD_DIM - nb, MOBA_BLOCK), jnp.bfloat16)
    qa = jnp.concatenate([qT[:HEAD_DIM], bias, pad], axis=0)

    s = jnp.dot(k_ref[0, i], qa, preferred_element_type=jnp.float32)
    kpos = lax.broadcasted_iota(jnp.int32, s.shape, 0)
    qpos = lax.broadcasted_iota(jnp.int32, s.shape, 1)
    s = jnp.where(kpos <= qpos, s, -jnp.inf)
    m = jnp.max(s, axis=0, keepdims=True)
    p = jnp.exp(s - m).astype(jnp.bfloat16)
    acc = jnp.dot(vT_ref[0, i], p, preferred_element_type=jnp.float32)

    def body(n, carry):
        m, acc = carry
        s = jnp.dot(k_ref[0, n], qa, preferred_element_type=jnp.float32)
        m_new = jnp.maximum(m, jnp.max(s, axis=0, keepdims=True))
        alpha = jnp.exp(m - m_new)
        p = jnp.exp(s - m_new).astype(jnp.bfloat16)
        acc = acc * alpha + jnp.dot(vT_ref[0, n], p, preferred_element_type=jnp.float32)
        return m_new, acc

    m, acc = lax.fori_loop(0, i, body, (m, acc))
    o_ref[0] = (acc[:HEAD_DIM] * (1.0 / acc[HEAD_DIM:HEAD_DIM + 1])).astype(o_ref.dtype)


def _moba_attention(qT, k, vT, kmean):
    B, _, S = qT.shape
    nb = S // MOBA_BLOCK
    return pl.pallas_call(
        _moba_kernel,
        grid=(B, N_HEADS, nb),
        in_specs=[
            pl.BlockSpec((1, Q_ROWS, MOBA_BLOCK), lambda b, h, i: (b, h, i)),
            pl.BlockSpec((1, nb, MOBA_BLOCK, K_LANES), lambda b, h, i: (b, 0, 0, h)),
            pl.BlockSpec((1, nb, V_ROWS, MOBA_BLOCK), lambda b, h, i: (b, 0, h, 0)),
            pl.BlockSpec((1, nb, K_LANES), lambda b, h, i: (b, 0, h)),
        ],
        out_specs=pl.BlockSpec((1, HEAD_DIM, MOBA_BLOCK), lambda b, h, i: (b, h, i)),
        out_shape=jax.ShapeDtypeStruct((B, ATTN_WIDTH, S), jnp.bfloat16),
        compiler_params=pltpu.CompilerParams(
            dimension_semantics=("parallel", "parallel", "arbitrary"), vmem_limit_bytes=VMEM_LIMIT),
        name="moba_attn",
    )(qT, k, vT, kmean)


CONV_TILE = 256
CONV_HALO = 32
CONV_ROWS = 64
LANES = 128


def _conv_kernel(cur_ref, prev_ref, w_ref, bdw_ref, gln_ref, bln_ref, o_ref, win_ref, y_ref):
    t = pl.program_id(1)
    prev = prev_ref[0]
    win_ref[0:CONV_HALO, :] = jnp.where(t > 0, prev, jnp.zeros_like(prev))
    win_ref[CONV_HALO:, :] = cur_ref[0]
    off = CONV_HALO - (CONV_K - 1)
    for c in range(CONV_WIDTH // LANES):
        lanes = slice(c * LANES, (c + 1) * LANES)
        for r in range(CONV_TILE // CONV_ROWS):
            acc = jnp.zeros((CONV_ROWS, LANES), jnp.float32)
            for j in range(CONV_K):
                acc = acc + win_ref[pl.ds(r * CONV_ROWS + off + j, CONV_ROWS), lanes] * w_ref[j:j + 1, lanes]
            y_ref[r * CONV_ROWS:(r + 1) * CONV_ROWS, lanes] = acc
    y = y_ref[...] + bdw_ref[...]
    mu = jnp.mean(y, axis=-1, keepdims=True)
    d = y - mu
    var = jnp.mean(d * d, axis=-1, keepdims=True)
    z = d * lax.rsqrt(var + EPS) * gln_ref[...] + bln_ref[...]
    o_ref[0] = (z * _sigmoid(z)).astype(o_ref.dtype)


def _conv_ln(hglu, w_dw, b_dw, g_ln, b_ln):
    B, S, _ = hglu.shape
    per = CONV_TILE // CONV_HALO
    const = lambda shape: pl.BlockSpec(shape, lambda b, t: (0,) * len(shape))
    return pl.pallas_call(
        _conv_kernel,
        grid=(B, S // CONV_TILE),
        in_specs=[
            pl.BlockSpec((1, CONV_TILE, CONV_WIDTH), lambda b, t: (b, t, 0)),
            pl.BlockSpec((1, CONV_HALO, CONV_WIDTH), lambda b, t: (b, jnp.maximum(t * per - 1, 0), 0)),
            const(w_dw.shape), const((1, CONV_WIDTH)), const((1, CONV_WIDTH)), const((1, CONV_WIDTH)),
        ],
        out_specs=pl.BlockSpec((1, CONV_TILE, CONV_WIDTH), lambda b, t: (b, t, 0)),
        out_shape=jax.ShapeDtypeStruct((B, S, CONV_WIDTH), jnp.bfloat16),
        scratch_shapes=[
            pltpu.VMEM((CONV_HALO + CONV_TILE, CONV_WIDTH), jnp.float32),
            pltpu.VMEM((CONV_TILE, CONV_WIDTH), jnp.float32),
        ],
        compiler_params=pltpu.CompilerParams(
            dimension_semantics=("parallel", "arbitrary"), vmem_limit_bytes=VMEM_LIMIT),
        name="conv_ln",
    )(hglu, hglu, w_dw, b_dw, g_ln, b_ln)


MLP_TILE = 512
FF_CHUNK = 1024


def _out_mlp_kernel(x_ref, aT_ref, cv_ref, woa_ref, woc_ref, g_ref, w1_ref, w2_ref, gf_ref, o_ref,
                    h_ref, *, final_norm):
    attn = aT_ref[0].T
    h = x_ref[...]
    h = h + jnp.dot(attn, woa_ref[...], preferred_element_type=jnp.float32)
    h = h + jnp.dot(cv_ref[...], woc_ref[...], preferred_element_type=jnp.float32)
    h_ref[...] = h
    hn = _rms(h, g_ref[...]).astype(jnp.bfloat16)

    def chunk(c, carry):
        ff = jnp.maximum(jnp.dot(hn, w1_ref[c], preferred_element_type=jnp.float32), 0.0)
        h_ref[...] += jnp.dot((ff * ff).astype(jnp.bfloat16), w2_ref[c], preferred_element_type=jnp.float32)
        return carry

    lax.fori_loop(0, D_FF // FF_CHUNK, chunk, 0)
    h = h_ref[...]
    o_ref[...] = _rms(h, gf_ref[...]) if final_norm else h


def _out_mlp(x2d, attnT, conv2d, woa, woc, g_mlp, w1, w2, g_final, final_norm):
    T = x2d.shape[0]
    S = attnT.shape[2]
    per = S // MLP_TILE
    const = lambda shape: pl.BlockSpec(shape, lambda t: (0,) * len(shape), pipeline_mode=pl.Buffered(1))
    return pl.pallas_call(
        functools.partial(_out_mlp_kernel, final_norm=final_norm),
        grid=(T // MLP_TILE,),
        in_specs=[
            pl.BlockSpec((MLP_TILE, D_MODEL), lambda t: (t, 0)),
            pl.BlockSpec((1, ATTN_WIDTH, MLP_TILE), lambda t: (t // per, 0, t % per)),
            pl.BlockSpec((MLP_TILE, CONV_WIDTH), lambda t: (t, 0)),
            const(woa.shape), const(woc.shape), const((1, D_MODEL)),
            const(w1.shape), const(w2.shape), const((1, D_MODEL)),
        ],
        out_specs=pl.BlockSpec((MLP_TILE, D_MODEL), lambda t: (t, 0)),
        out_shape=jax.ShapeDtypeStruct((T, D_MODEL), jnp.float32),
        scratch_shapes=[pltpu.VMEM((MLP_TILE, D_MODEL), jnp.float32)],
        compiler_params=pltpu.CompilerParams(
            dimension_semantics=("parallel",), vmem_limit_bytes=VMEM_LIMIT),
        name="out_mlp",
    )(x2d, attnT, conv2d, woa, woc, g_mlp, w1, w2, g_final)


def _rope_tables(S):
    inv_freq = ROPE_THETA ** (-jnp.arange(ROT_HALF, dtype=jnp.float32) * 2.0 / ROT_DIM)
    ang = jnp.arange(S, dtype=jnp.float32)[:, None] * inv_freq[None, :]
    cos, sin = jnp.cos(ang), jnp.sin(ang)
    zeros = jnp.zeros((S, K_LANES - ROT_DIM), jnp.float32)
    zero8 = jnp.zeros((S, ROT_HALF), jnp.float32)
    c = jnp.concatenate([cos, cos, jnp.ones_like(zeros)], axis=1)
    sa = jnp.concatenate([-sin, zero8, zeros], axis=1)
    sb = jnp.concatenate([zero8, sin, zeros], axis=1)
    return cos.T, sin.T, c, sa, sb


def _split_w_in(w):
    bf = jnp.bfloat16
    wq = w[:, :ATTN_WIDTH].T.reshape(N_HEADS, HEAD_DIM, D_MODEL) * (HEAD_DIM ** -0.5)
    wq = jnp.pad(wq, ((0, 0), (0, Q_ROWS - HEAD_DIM), (0, 0))).reshape(N_HEADS * Q_ROWS, D_MODEL)
    wk = w[:, ATTN_WIDTH:2 * ATTN_WIDTH].reshape(D_MODEL, N_HEADS, HEAD_DIM)
    wk = jnp.pad(wk, ((0, 0), (0, 0), (0, K_LANES - HEAD_DIM))).reshape(D_MODEL, N_HEADS * K_LANES)
    wv = w[:, 2 * ATTN_WIDTH:3 * ATTN_WIDTH].T.reshape(N_HEADS, HEAD_DIM, D_MODEL)
    wv = jnp.pad(wv, ((0, 0), (0, V_ROWS - HEAD_DIM), (0, 0))).reshape(N_HEADS * V_ROWS, D_MODEL)
    wu = w[:, 3 * ATTN_WIDTH:]
    return wq.astype(bf), wk.astype(bf), wv.astype(bf), wu.astype(bf)


def kernel(x, g_mix_norm, w_in, b_glu, w_dw, b_dw, g_conv_ln, b_conv_ln, w_out, g_mlp_norm,
           w_mlp_in, w_mlp_out, g_final):
    B, S, D = x.shape
    depth = w_in.shape[0]
    bf = jnp.bfloat16
    tables = _rope_tables(S)
    h = x
    for l in range(depth):
        wq, wk, wv, wu = _split_w_in(w_in[l])
        qT, k, vT, kmean, hglu = _in_proj(
            h, g_mix_norm[l][None], wq, wk, wv, wu, b_glu[l][None], *tables)
        attnT = _moba_attention(qT, k, vT, kmean)
        conv = _conv_ln(hglu, w_dw[l][:, 0, :], b_dw[l][None], g_conv_ln[l][None], b_conv_ln[l][None])
        last = l == depth - 1
        h = _out_mlp(
            h.reshape(B * S, D), attnT, conv.reshape(B * S, CONV_WIDTH),
            w_out[l][:ATTN_WIDTH].astype(bf), w_out[l][ATTN_WIDTH:].astype(bf), g_mlp_norm[l][None],
            w_mlp_in[l].reshape(D, D_FF // FF_CHUNK, FF_CHUNK).transpose(1, 0, 2).astype(bf),
            w_mlp_out[l].reshape(D_FF // FF_CHUNK, FF_CHUNK, D).astype(bf), g_final[None], final_norm=last,
        ).reshape(B, S, D)
    return h
```

```python
import functools

import jax
import jax.numpy as jnp
from jax import lax
from jax.experimental import pallas as pl
from jax.experimental.pallas import tpu as pltpu

D_MODEL = 1024
ATTN_WIDTH = 512
CONV_WIDTH = 512
HEAD_DIM = 64
N_HEADS = 8
ROT_DIM = 16
ROT_HALF = 8
ROPE_THETA = 500000.0
MOBA_BLOCK = 256
MOBA_TOP_K = 3
CONV_K = 31
D_FF = 4096
EPS = 1e-6

Q_ROWS = 128
K_LANES = 128
BIAS_ROW0 = HEAD_DIM
V_ROWS = 80
MASKED = -1e30
LOG2E = 1.4426950408889634

VMEM_LIMIT = 56 * 1024 * 1024

_NT = (((1,), (1,)), ((), ()))


def _rms(x, g):
    return x * lax.rsqrt(jnp.mean(x * x, axis=-1, keepdims=True) + EPS) * g


def _sigmoid(x):
    return 1.0 / (1.0 + jnp.exp(-x))


def _in_proj_kernel(x_ref, g_ref, wq_ref, wk_ref, wv_ref, wu_ref, bglu_ref,
                    cosT_ref, sinT_ref, c_ref, sa_ref, sb_ref,
                    qT_ref, k_ref, vT_ref, kmean_ref, hglu_ref):
    n = pl.program_id(1)
    xb = _rms(x_ref[0], g_ref[...]).astype(jnp.bfloat16)

    qT = lax.dot_general(wq_ref[...], xb, _NT, preferred_element_type=jnp.float32)
    q3 = qT.reshape(N_HEADS, Q_ROWS, MOBA_BLOCK)
    cos = cosT_ref[...][None]
    sin = sinT_ref[...][None]
    x1 = q3[:, 0:ROT_HALF]
    x2 = q3[:, ROT_HALF:ROT_DIM]
    q3 = jnp.concatenate([x1 * cos - x2 * sin, x2 * cos + x1 * sin, q3[:, ROT_DIM:]], axis=1)
    qT_ref[0] = (q3 * LOG2E).reshape(N_HEADS * Q_ROWS, MOBA_BLOCK).astype(jnp.bfloat16)

    k = jnp.dot(xb, wk_ref[...], preferred_element_type=jnp.float32)
    c = c_ref[...]
    sa = sa_ref[...]
    sb = sb_ref[...]
    parts = []
    for h in range(N_HEADS):
        kh = k[:, h * K_LANES:(h + 1) * K_LANES]
        parts.append(kh * c + pltpu.roll(kh, K_LANES - ROT_HALF, 1) * sa + pltpu.roll(kh, ROT_HALF, 1) * sb)
    k = jnp.concatenate(parts, axis=1)
    kmean_ref[0, pl.ds(n, 1), :] = jnp.mean(k, axis=0, keepdims=True)
    lane = lax.broadcasted_iota(jnp.int32, k.shape, 1) % K_LANES
    k_ref[0, 0] = jnp.where(lane == BIAS_ROW0 + n, 1.0, k).astype(jnp.bfloat16)

    vT = lax.dot_general(wv_ref[...], xb, _NT, preferred_element_type=jnp.float32)
    row = lax.broadcasted_iota(jnp.int32, vT.shape, 0) % V_ROWS
    vT_ref[0, 0] = jnp.where(row == HEAD_DIM, 1.0, vT).astype(jnp.bfloat16)

    u = jnp.dot(xb, wu_ref[...], preferred_element_type=jnp.float32) + bglu_ref[...]
    hglu_ref[0] = u[:, :CONV_WIDTH] * _sigmoid(u[:, CONV_WIDTH:])


def _in_proj(x, g, wq, wk, wv, wu, bglu, cosT, sinT, c, sa, sb):
    B, S, _ = x.shape
    nb = S // MOBA_BLOCK
    const = lambda shape: pl.BlockSpec(shape, lambda b, n: (0,) * len(shape))
    return pl.pallas_call(
        _in_proj_kernel,
        grid=(B, nb),
        in_specs=[
            pl.BlockSpec((1, MOBA_BLOCK, D_MODEL), lambda b, n: (b, n, 0)),
            const((1, D_MODEL)),
            const(wq.shape), const(wk.shape), const(wv.shape), const(wu.shape),
            const((1, 2 * CONV_WIDTH)),
            pl.BlockSpec((ROT_HALF, MOBA_BLOCK), lambda b, n: (0, n)),
            pl.BlockSpec((ROT_HALF, MOBA_BLOCK), lambda b, n: (0, n)),
            pl.BlockSpec((MOBA_BLOCK, K_LANES), lambda b, n: (n, 0)),
            pl.BlockSpec((MOBA_BLOCK, K_LANES), lambda b, n: (n, 0)),
            pl.BlockSpec((MOBA_BLOCK, K_LANES), lambda b, n: (n, 0)),
        ],
        out_specs=[
            pl.BlockSpec((1, N_HEADS * Q_ROWS, MOBA_BLOCK), lambda b, n: (b, 0, n)),
            pl.BlockSpec((1, 1, MOBA_BLOCK, N_HEADS * K_LANES), lambda b, n: (b, n, 0, 0)),
            pl.BlockSpec((1, 1, N_HEADS * V_ROWS, MOBA_BLOCK), lambda b, n: (b, n, 0, 0)),
            pl.BlockSpec((1, nb, N_HEADS * K_LANES), lambda b, n: (b, 0, 0)),
            pl.BlockSpec((1, MOBA_BLOCK, CONV_WIDTH), lambda b, n: (b, n, 0)),
        ],
        out_shape=[
            jax.ShapeDtypeStruct((B, N_HEADS * Q_ROWS, S), jnp.bfloat16),
            jax.ShapeDtypeStruct((B, nb, MOBA_BLOCK, N_HEADS * K_LANES), jnp.bfloat16),
            jax.ShapeDtypeStruct((B, nb, N_HEADS * V_ROWS, MOBA_BLOCK), jnp.bfloat16),
            jax.ShapeDtypeStruct((B, nb, N_HEADS * K_LANES), jnp.float32),
            jax.ShapeDtypeStruct((B, S, CONV_WIDTH), jnp.float32),
        ],
        compiler_params=pltpu.CompilerParams(
            dimension_semantics=("parallel", "arbitrary"), vmem_limit_bytes=VMEM_LIMIT),
        name="in_proj",
    )(x, g, wq, wk, wv, wu, bglu, cosT, sinT, c, sa, sb)


HEADS_PER_STEP = 4


def _moba_kernel(qT_ref, k_ref, vT_ref, kmean_ref, o_ref):
    i = pl.program_id(2)
    nb = k_ref.shape[1]
    kpos = lax.broadcasted_iota(jnp.int32, (MOBA_BLOCK, MOBA_BLOCK), 0)
    qpos = lax.broadcasted_iota(jnp.int32, (MOBA_BLOCK, MOBA_BLOCK), 1)
    causal = kpos <= qpos
    blk = lax.broadcasted_iota(jnp.int32, (nb, MOBA_BLOCK), 0)
    pad = jnp.zeros((Q_ROWS - HEAD_DIM - nb, MOBA_BLOCK), jnp.bfloat16)

    def stage_logits(h, n, qa):
        k_blk = k_ref[0, n, :, h * K_LANES:(h + 1) * K_LANES]
        return jnp.dot(k_blk, qa, preferred_element_type=jnp.float32)

    def stage_softmax(s, m, own):
        if own:
            s = jnp.where(causal, s, -jnp.inf)
        m_blk = jnp.max(s, axis=0, keepdims=True)
        m_new = m_blk if m is None else jnp.maximum(m, m_blk)
        return m_new, jnp.exp2(s - m_new).astype(jnp.bfloat16)

    def stage_pv(h, n, p, m, m_new, acc):
        vT_blk = vT_ref[0, n, h * V_ROWS:(h + 1) * V_ROWS, :]
        pv = jnp.dot(vT_blk, p, preferred_element_type=jnp.float32)
        return pv if acc is None else pv + acc * jnp.exp2(m - m_new)

    def attend_all(n, qas, carry, own):
        hp = HEADS_PER_STEP
        s = [None] * hp
        new = [None] * hp
        s[0] = stage_logits(0, n, qas[0])
        for h in range(hp):
            if h + 1 < hp:
                s[h + 1] = stage_logits(h + 1, n, qas[h + 1])
            m, acc = carry[h]
            m_new, p = stage_softmax(s[h], m, own)
            new[h] = (m_new, stage_pv(h, n, p, m, m_new, acc))
        return tuple(new)

    qas = []
    for h in range(HEADS_PER_STEP):
        qT = qT_ref[0, h * Q_ROWS:(h + 1) * Q_ROWS, :]

        km = kmean_ref[0, :, h * K_LANES:(h + 1) * K_LANES]
        km_hi = km.astype(jnp.bfloat16)
        km_lo = (km - km_hi.astype(jnp.float32)).astype(jnp.bfloat16)
        g2 = jnp.dot(jnp.concatenate([km_hi, km_lo], axis=0), qT, preferred_element_type=jnp.float32)
        gate = g2[:nb] + g2[nb:]

        rem = jnp.where(blk < i, gate, -jnp.inf)
        sel = blk == i
        for _ in range(MOBA_TOP_K):
            mx = jnp.max(rem, axis=0, keepdims=True)
            first = jnp.min(jnp.where(rem == mx, blk, nb), axis=0, keepdims=True)
            pick = (blk == first) & (mx > -jnp.inf)
            sel = sel | pick
            rem = jnp.where(pick, -jnp.inf, rem)
        bias = jnp.where(sel, 0.0, MASKED).astype(jnp.bfloat16)

        qa = jnp.concatenate([qT[:HEAD_DIM], bias, pad], axis=0)
        qas.append(qa)

    carry = attend_all(i, qas, ((None, None),) * HEADS_PER_STEP, own=True)
    carry = lax.fori_loop(0, i, lambda n, c: attend_all(n, qas, c, own=False), carry)
    for h in range(HEADS_PER_STEP):
        acc = carry[h][1]
        o_ref[0, h * HEAD_DIM:(h + 1) * HEAD_DIM, :] = (
            acc[:HEAD_DIM] * (1.0 / acc[HEAD_DIM:HEAD_DIM + 1])).astype(o_ref.dtype)


def _moba_attention(qT, k, vT, kmean):
    B, _, S = qT.shape
    nb = S // MOBA_BLOCK
    hp = HEADS_PER_STEP
    return pl.pallas_call(
        _moba_kernel,
        grid=(B, N_HEADS // hp, nb),
        in_specs=[
            pl.BlockSpec((1, hp * Q_ROWS, MOBA_BLOCK), lambda b, g, i: (b, g, i)),
            pl.BlockSpec((1, nb, MOBA_BLOCK, hp * K_LANES), lambda b, g, i: (b, 0, 0, g)),
            pl.BlockSpec((1, nb, hp * V_ROWS, MOBA_BLOCK), lambda b, g, i: (b, 0, g, 0)),
            pl.BlockSpec((1, nb, hp * K_LANES), lambda b, g, i: (b, 0, g)),
        ],
        out_specs=pl.BlockSpec((1, hp * HEAD_DIM, MOBA_BLOCK), lambda b, g, i: (b, g, i)),
        out_shape=jax.ShapeDtypeStruct((B, ATTN_WIDTH, S), jnp.bfloat16),
        compiler_params=pltpu.CompilerParams(
            dimension_semantics=("parallel", "parallel", "arbitrary"), vmem_limit_bytes=VMEM_LIMIT),
        name="moba_attn",
    )(qT, k, vT, kmean)


CONV_TILE = 256
CONV_HALO = 32
CONV_ROWS = 64
LANES = 128


def _conv_kernel(cur_ref, prev_ref, w_ref, bdw_ref, gln_ref, bln_ref, o_ref, win_ref, y_ref):
    t = pl.program_id(1)
    prev = prev_ref[0]
    win_ref[0:CONV_HALO, :] = jnp.where(t > 0, prev, jnp.zeros_like(prev))
    win_ref[CONV_HALO:, :] = cur_ref[0]
    off = CONV_HALO - (CONV_K - 1)
    for c in range(CONV_WIDTH // LANES):
        lanes = slice(c * LANES, (c + 1) * LANES)
        for r in range(CONV_TILE // CONV_ROWS):
            acc = jnp.zeros((CONV_ROWS, LANES), jnp.float32)
            for j in range(CONV_K):
                acc = acc + win_ref[pl.ds(r * CONV_ROWS + off + j, CONV_ROWS), lanes] * w_ref[j:j + 1, lanes]
            y_ref[r * CONV_ROWS:(r + 1) * CONV_ROWS, lanes] = acc
    y = y_ref[...] + bdw_ref[...]
    mu = jnp.mean(y, axis=-1, keepdims=True)
    d = y - mu
    var = jnp.mean(d * d, axis=-1, keepdims=True)
    z = d * lax.rsqrt(var + EPS) * gln_ref[...] + bln_ref[...]
    o_ref[0] = (z * _sigmoid(z)).astype(o_ref.dtype)


def _conv_ln(hglu, w_dw, b_dw, g_ln, b_ln):
    B, S, _ = hglu.shape
    per = CONV_TILE // CONV_HALO
    const = lambda shape: pl.BlockSpec(shape, lambda b, t: (0,) * len(shape))
    return pl.pallas_call(
        _conv_kernel,
        grid=(B, S // CONV_TILE),
        in_specs=[
            pl.BlockSpec((1, CONV_TILE, CONV_WIDTH), lambda b, t: (b, t, 0)),
            pl.BlockSpec((1, CONV_HALO, CONV_WIDTH), lambda b, t: (b, jnp.maximum(t * per - 1, 0), 0)),
            const(w_dw.shape), const((1, CONV_WIDTH)), const((1, CONV_WIDTH)), const((1, CONV_WIDTH)),
        ],
        out_specs=pl.BlockSpec((1, CONV_TILE, CONV_WIDTH), lambda b, t: (b, t, 0)),
        out_shape=jax.ShapeDtypeStruct((B, S, CONV_WIDTH), jnp.bfloat16),
        scratch_shapes=[
            pltpu.VMEM((CONV_HALO + CONV_TILE, CONV_WIDTH), jnp.float32),
            pltpu.VMEM((CONV_TILE, CONV_WIDTH), jnp.float32),
        ],
        compiler_params=pltpu.CompilerParams(
            dimension_semantics=("parallel", "arbitrary"), vmem_limit_bytes=VMEM_LIMIT),
        name="conv_ln",
    )(hglu, hglu, w_dw, b_dw, g_ln, b_ln)


MLP_TILE = 512
FF_CHUNK = 1024


def _out_mlp_kernel(x_ref, aT_ref, cv_ref, woa_ref, woc_ref, g_ref, w1_ref, w2_ref, gf_ref, o_ref,
                    h_ref, *, final_norm):
    attn = aT_ref[0].T
    h = x_ref[...]
    h = h + jnp.dot(attn, woa_ref[...], preferred_element_type=jnp.float32)
    h = h + jnp.dot(cv_ref[...], woc_ref[...], preferred_element_type=jnp.float32)
    h_ref[...] = h
    hn = _rms(h, g_ref[...]).astype(jnp.bfloat16)

    def chunk(c, carry):
        ff = jnp.maximum(jnp.dot(hn, w1_ref[c], preferred_element_type=jnp.float32), 0.0)
        h_ref[...] += jnp.dot((ff * ff).astype(jnp.bfloat16), w2_ref[c], preferred_element_type=jnp.float32)
        return carry

    lax.fori_loop(0, D_FF // FF_CHUNK, chunk, 0)
    h = h_ref[...]
    o_ref[...] = _rms(h, gf_ref[...]) if final_norm else h


def _out_mlp(x2d, attnT, conv2d, woa, woc, g_mlp, w1, w2, g_final, final_norm):
    T = x2d.shape[0]
    S = attnT.shape[2]
    per = S // MLP_TILE
    const = lambda shape: pl.BlockSpec(shape, lambda t: (0,) * len(shape), pipeline_mode=pl.Buffered(1))
    return pl.pallas_call(
        functools.partial(_out_mlp_kernel, final_norm=final_norm),
        grid=(T // MLP_TILE,),
        in_specs=[
            pl.BlockSpec((MLP_TILE, D_MODEL), lambda t: (t, 0)),
            pl.BlockSpec((1, ATTN_WIDTH, MLP_TILE), lambda t: (t // per, 0, t % per)),
            pl.BlockSpec((MLP_TILE, CONV_WIDTH), lambda t: (t, 0)),
            const(woa.shape), const(woc.shape), const((1, D_MODEL)),
            const(w1.shape), const(w2.shape), const((1, D_MODEL)),
        ],
        out_specs=pl.BlockSpec((MLP_TILE, D_MODEL), lambda t: (t, 0)),
        out_shape=jax.ShapeDtypeStruct((T, D_MODEL), jnp.float32),
        scratch_shapes=[pltpu.VMEM((MLP_TILE, D_MODEL), jnp.float32)],
        compiler_params=pltpu.CompilerParams(
            dimension_semantics=("parallel",), vmem_limit_bytes=VMEM_LIMIT),
        name="out_mlp",
    )(x2d, attnT, conv2d, woa, woc, g_mlp, w1, w2, g_final)


def _rope_tables(S):
    inv_freq = ROPE_THETA ** (-jnp.arange(ROT_HALF, dtype=jnp.float32) * 2.0 / ROT_DIM)
    ang = jnp.arange(S, dtype=jnp.float32)[:, None] * inv_freq[None, :]
    cos, sin = jnp.cos(ang), jnp.sin(ang)
    zeros = jnp.zeros((S, K_LANES - ROT_DIM), jnp.float32)
    zero8 = jnp.zeros((S, ROT_HALF), jnp.float32)
    c = jnp.concatenate([cos, cos, jnp.ones_like(zeros)], axis=1)
    sa = jnp.concatenate([-sin, zero8, zeros], axis=1)
    sb = jnp.concatenate([zero8, sin, zeros], axis=1)
    return cos.T, sin.T, c, sa, sb


def _split_w_in(w):
    bf = jnp.bfloat16
    wq = w[:, :ATTN_WIDTH].T.reshape(N_HEADS, HEAD_DIM, D_MODEL) * (HEAD_DIM ** -0.5)
    wq = jnp.pad(wq, ((0, 0), (0, Q_ROWS - HEAD_DIM), (0, 0))).reshape(N_HEADS * Q_ROWS, D_MODEL)
    wk = w[:, ATTN_WIDTH:2 * ATTN_WIDTH].reshape(D_MODEL, N_HEADS, HEAD_DIM)
    wk = jnp.pad(wk, ((0, 0), (0, 0), (0, K_LANES - HEAD_DIM))).reshape(D_MODEL, N_HEADS * K_LANES)
    wv = w[:, 2 * ATTN_WIDTH:3 * ATTN_WIDTH].T.reshape(N_HEADS, HEAD_DIM, D_MODEL)
    wv = jnp.pad(wv, ((0, 0), (0, V_ROWS - HEAD_DIM), (0, 0))).reshape(N_HEADS * V_ROWS, D_MODEL)
    wu = w[:, 3 * ATTN_WIDTH:]
    return wq.astype(bf), wk.astype(bf), wv.astype(bf), wu.astype(bf)


def kernel(x, g_mix_norm, w_in, b_glu, w_dw, b_dw, g_conv_ln, b_conv_ln, w_out, g_mlp_norm,
           w_mlp_in, w_mlp_out, g_final):
    B, S, D = x.shape
    depth = w_in.shape[0]
    bf = jnp.bfloat16
    tables = _rope_tables(S)
    h = x
    for l in range(depth):
        wq, wk, wv, wu = _split_w_in(w_in[l])
        qT, k, vT, kmean, hglu = _in_proj(
            h, g_mix_norm[l][None], wq, wk, wv, wu, b_glu[l][None], *tables)
        attnT = _moba_attention(qT, k, vT, kmean)
        conv = _conv_ln(hglu, w_dw[l][:, 0, :], b_dw[l][None], g_conv_ln[l][None], b_conv_ln[l][None])
        last = l == depth - 1
        h = _out_mlp(
            h.reshape(B * S, D), attnT, conv.reshape(B * S, CONV_WIDTH),
            w_out[l][:ATTN_WIDTH].astype(bf), w_out[l][ATTN_WIDTH:].astype(bf), g_mlp_norm[l][None],
            w_mlp_in[l].reshape(D, D_FF // FF_CHUNK, FF_CHUNK).transpose(1, 0, 2).astype(bf),
            w_mlp_out[l].reshape(D_FF // FF_CHUNK, FF_CHUNK, D).astype(bf), g_final[None], final_norm=last,
        ).reshape(B, S, D)
    return h
```

```python
import functools

import jax
import jax.numpy as jnp
from jax import lax
from jax.experimental import pallas as pl
from jax.experimental.pallas import tpu as pltpu

D_MODEL = 1024
ATTN_WIDTH = 512
CONV_WIDTH = 512
HEAD_DIM = 64
N_HEADS = 8
ROT_DIM = 16
ROT_HALF = 8
ROPE_THETA = 500000.0
MOBA_BLOCK = 256
MOBA_TOP_K = 3
CONV_K = 31
D_FF = 4096
EPS = 1e-6

Q_ROWS = 128
K_LANES = 128
BIAS_ROW0 = HEAD_DIM
V_ROWS = 80
MASKED = -1e30
LOG2E = 1.4426950408889634

VMEM_LIMIT = 56 * 1024 * 1024

_NT = (((1,), (1,)), ((), ()))


def _rms(x, g):
    return x * lax.rsqrt(jnp.mean(x * x, axis=-1, keepdims=True) + EPS) * g


def _sigmoid(x):
    return 1.0 / (1.0 + jnp.exp(-x))


def _in_proj_kernel(x_ref, g_ref, wq_ref, wk_ref, wv_ref, wu_ref, bglu_ref,
                    cosT_ref, sinT_ref, c_ref, sa_ref, sb_ref,
                    qT_ref, k_ref, vT_ref, kmean_ref, hglu_ref):
    n = pl.program_id(1)
    xb = _rms(x_ref[0], g_ref[...]).astype(jnp.bfloat16)

    qT = lax.dot_general(wq_ref[...], xb, _NT, preferred_element_type=jnp.float32)
    q3 = qT.reshape(N_HEADS, Q_ROWS, MOBA_BLOCK)
    cos = cosT_ref[...][None]
    sin = sinT_ref[...][None]
    x1 = q3[:, 0:ROT_HALF]
    x2 = q3[:, ROT_HALF:ROT_DIM]
    q3 = jnp.concatenate([x1 * cos - x2 * sin, x2 * cos + x1 * sin, q3[:, ROT_DIM:]], axis=1)
    qT_ref[0] = (q3 * LOG2E).reshape(N_HEADS * Q_ROWS, MOBA_BLOCK).astype(jnp.bfloat16)

    k = jnp.dot(xb, wk_ref[...], preferred_element_type=jnp.float32)
    c = c_ref[...]
    sa = sa_ref[...]
    sb = sb_ref[...]
    parts = []
    for h in range(N_HEADS):
        kh = k[:, h * K_LANES:(h + 1) * K_LANES]
        parts.append(kh * c + pltpu.roll(kh, K_LANES - ROT_HALF, 1) * sa + pltpu.roll(kh, ROT_HALF, 1) * sb)
    k = jnp.concatenate(parts, axis=1)
    kmean_ref[0, pl.ds(n, 1), :] = jnp.mean(k, axis=0, keepdims=True)
    lane = lax.broadcasted_iota(jnp.int32, k.shape, 1) % K_LANES
    k_ref[0, 0] = jnp.where(lane == BIAS_ROW0 + n, 1.0, k).astype(jnp.bfloat16)

    vT = lax.dot_general(wv_ref[...], xb, _NT, preferred_element_type=jnp.float32)
    row = lax.broadcasted_iota(jnp.int32, vT.shape, 0) % V_ROWS
    vT_ref[0, 0] = jnp.where(row == HEAD_DIM, 1.0, vT).astype(jnp.bfloat16)

    u = jnp.dot(xb, wu_ref[...], preferred_element_type=jnp.float32) + bglu_ref[...]
    hglu_ref[0] = u[:, :CONV_WIDTH] * _sigmoid(u[:, CONV_WIDTH:])


def _in_proj(x, g, wq, wk, wv, wu, bglu, cosT, sinT, c, sa, sb):
    B, S, _ = x.shape
    nb = S // MOBA_BLOCK
    const = lambda shape: pl.BlockSpec(shape, lambda b, n: (0,) * len(shape))
    return pl.pallas_call(
        _in_proj_kernel,
        grid=(B, nb),
        in_specs=[
            pl.BlockSpec((1, MOBA_BLOCK, D_MODEL), lambda b, n: (b, n, 0)),
            const((1, D_MODEL)),
            const(wq.shape), const(wk.shape), const(wv.shape), const(wu.shape),
            const((1, 2 * CONV_WIDTH)),
            pl.BlockSpec((ROT_HALF, MOBA_BLOCK), lambda b, n: (0, n)),
            pl.BlockSpec((ROT_HALF, MOBA_BLOCK), lambda b, n: (0, n)),
            pl.BlockSpec((MOBA_BLOCK, K_LANES), lambda b, n: (n, 0)),
            pl.BlockSpec((MOBA_BLOCK, K_LANES), lambda b, n: (n, 0)),
            pl.BlockSpec((MOBA_BLOCK, K_LANES), lambda b, n: (n, 0)),
        ],
        out_specs=[
            pl.BlockSpec((1, N_HEADS * Q_ROWS, MOBA_BLOCK), lambda b, n: (b, 0, n)),
            pl.BlockSpec((1, 1, MOBA_BLOCK, N_HEADS * K_LANES), lambda b, n: (b, n, 0, 0)),
            pl.BlockSpec((1, 1, N_HEADS * V_ROWS, MOBA_BLOCK), lambda b, n: (b, n, 0, 0)),
            pl.BlockSpec((1, nb, N_HEADS * K_LANES), lambda b, n: (b, 0, 0)),
            pl.BlockSpec((1, MOBA_BLOCK, CONV_WIDTH), lambda b, n: (b, n, 0)),
        ],
        out_shape=[
            jax.ShapeDtypeStruct((B, N_HEADS * Q_ROWS, S), jnp.bfloat16),
            jax.ShapeDtypeStruct((B, nb, MOBA_BLOCK, N_HEADS * K_LANES), jnp.bfloat16),
            jax.ShapeDtypeStruct((B, nb, N_HEADS * V_ROWS, MOBA_BLOCK), jnp.bfloat16),
            jax.ShapeDtypeStruct((B, nb, N_HEADS * K_LANES), jnp.float32),
            jax.ShapeDtypeStruct((B, S, CONV_WIDTH), jnp.float32),
        ],
        compiler_params=pltpu.CompilerParams(
            dimension_semantics=("parallel", "arbitrary"), vmem_limit_bytes=VMEM_LIMIT),
        name="in_proj",
    )(x, g, wq, wk, wv, wu, bglu, cosT, sinT, c, sa, sb)


HEADS_PER_STEP = 4


def _moba_kernel(qT_ref, k_ref, vT_ref, kmean_ref, o_ref, s_ref, p_ref):
    i = pl.program_id(2)
    nb = k_ref.shape[1]
    kpos = lax.broadcasted_iota(jnp.int32, (MOBA_BLOCK, MOBA_BLOCK), 0)
    qpos = lax.broadcasted_iota(jnp.int32, (MOBA_BLOCK, MOBA_BLOCK), 1)
    causal = kpos <= qpos
    blk = lax.broadcasted_iota(jnp.int32, (nb, MOBA_BLOCK), 0)
    pad = jnp.zeros((Q_ROWS - HEAD_DIM - nb, MOBA_BLOCK), jnp.bfloat16)

    def stage_logits(h, n, qa):
        k_blk = k_ref[0, n, :, h * K_LANES:(h + 1) * K_LANES]
        return jnp.dot(k_blk, qa, preferred_element_type=jnp.float32)

    def stage_values(h, n, p):
        vT_blk = vT_ref[0, n, h * V_ROWS:(h + 1) * V_ROWS, :]
        return jnp.dot(vT_blk, p, preferred_element_type=jnp.float32)

    def stage_softmax(s, m, own):
        if own:
            s = jnp.where(causal, s, -jnp.inf)
        m_new = jnp.maximum(m, jnp.max(s, axis=0, keepdims=True))
        return m_new, jnp.exp2(m - m_new), jnp.exp2(s - m_new).astype(jnp.bfloat16)

    qas = []
    for h in range(HEADS_PER_STEP):
        qT = qT_ref[0, h * Q_ROWS:(h + 1) * Q_ROWS, :]

        km = kmean_ref[0, :, h * K_LANES:(h + 1) * K_LANES]
        km_hi = km.astype(jnp.bfloat16)
        km_lo = (km - km_hi.astype(jnp.float32)).astype(jnp.bfloat16)
        g2 = jnp.dot(jnp.concatenate([km_hi, km_lo], axis=0), qT, preferred_element_type=jnp.float32)
        gate = g2[:nb] + g2[nb:]

        rem = jnp.where(blk < i, gate, -jnp.inf)
        sel = blk == i
        for _ in range(MOBA_TOP_K):
            mx = jnp.max(rem, axis=0, keepdims=True)
            first = jnp.min(jnp.where(rem == mx, blk, nb), axis=0, keepdims=True)
            pick = (blk == first) & (mx > -jnp.inf)
            sel = sel | pick
            rem = jnp.where(pick, -jnp.inf, rem)
        bias = jnp.where(sel, 0.0, MASKED).astype(jnp.bfloat16)

        qa = jnp.concatenate([qT[:HEAD_DIM], bias, pad], axis=0)
        qas.append(qa)

    heads = range(HEADS_PER_STEP)
    for h in heads:
        s_ref[h] = stage_logits(h, 0, qas[h])
        p_ref[h] = jnp.zeros((MOBA_BLOCK, MOBA_BLOCK), jnp.bfloat16)

    def values_behind(t, carry):
        pvs = [stage_values(h, jnp.maximum(t - 1, 0), p_ref[h]) for h in heads]
        return [acc * alpha + pv for (_, alpha, acc), pv in zip(carry, pvs)]

    def body(t, carry):
        ahead = [stage_logits(h, t + 1, qas[h]) for h in heads]
        accs = values_behind(t, carry)
        new = []
        for h in heads:
            m_new, alpha, p = stage_softmax(s_ref[h], carry[h][0], own=False)
            p_ref[h] = p
            new.append((m_new, alpha, accs[h]))
        for h in heads:
            s_ref[h] = ahead[h]
        return tuple(new)

    init = (jnp.full((1, MOBA_BLOCK), -jnp.inf, jnp.float32), jnp.ones((1, MOBA_BLOCK), jnp.float32),
            jnp.zeros((V_ROWS, MOBA_BLOCK), jnp.float32))
    carry = lax.fori_loop(0, i, body, (init,) * HEADS_PER_STEP)
    accs = values_behind(i, carry)
    for h in heads:
        _, alpha, p = stage_softmax(s_ref[h], carry[h][0], own=True)
        acc = accs[h] * alpha + stage_values(h, i, p)
        o_ref[0, h * HEAD_DIM:(h + 1) * HEAD_DIM, :] = (
            acc[:HEAD_DIM] * (1.0 / acc[HEAD_DIM:HEAD_DIM + 1])).astype(o_ref.dtype)


def _moba_attention(qT, k, vT, kmean):
    B, _, S = qT.shape
    nb = S // MOBA_BLOCK
    hp = HEADS_PER_STEP
    return pl.pallas_call(
        _moba_kernel,
        grid=(B, N_HEADS // hp, nb),
        in_specs=[
            pl.BlockSpec((1, hp * Q_ROWS, MOBA_BLOCK), lambda b, g, i: (b, g, i)),
            pl.BlockSpec((1, nb, MOBA_BLOCK, hp * K_LANES), lambda b, g, i: (b, 0, 0, g)),
            pl.BlockSpec((1, nb, hp * V_ROWS, MOBA_BLOCK), lambda b, g, i: (b, 0, g, 0)),
            pl.BlockSpec((1, nb, hp * K_LANES), lambda b, g, i: (b, 0, g)),
        ],
        out_specs=pl.BlockSpec((1, hp * HEAD_DIM, MOBA_BLOCK), lambda b, g, i: (b, g, i)),
        out_shape=jax.ShapeDtypeStruct((B, ATTN_WIDTH, S), jnp.bfloat16),
        scratch_shapes=[
            pltpu.VMEM((hp, MOBA_BLOCK, MOBA_BLOCK), jnp.float32),
            pltpu.VMEM((hp, MOBA_BLOCK, MOBA_BLOCK), jnp.bfloat16),
        ],
        compiler_params=pltpu.CompilerParams(
            dimension_semantics=("parallel", "parallel", "arbitrary"), vmem_limit_bytes=VMEM_LIMIT),
        name="moba_attn",
    )(qT, k, vT, kmean)


CONV_TILE = 256
CONV_HALO = 32
CONV_ROWS = 64
LANES = 128


def _conv_kernel(cur_ref, prev_ref, w_ref, bdw_ref, gln_ref, bln_ref, o_ref, win_ref, y_ref):
    t = pl.program_id(1)
    prev = prev_ref[0]
    win_ref[0:CONV_HALO, :] = jnp.where(t > 0, prev, jnp.zeros_like(prev))
    win_ref[CONV_HALO:, :] = cur_ref[0]
    off = CONV_HALO - (CONV_K - 1)
    for c in range(CONV_WIDTH // LANES):
        lanes = slice(c * LANES, (c + 1) * LANES)
        for r in range(CONV_TILE // CONV_ROWS):
            acc = jnp.zeros((CONV_ROWS, LANES), jnp.float32)
            for j in range(CONV_K):
                acc = acc + win_ref[pl.ds(r * CONV_ROWS + off + j, CONV_ROWS), lanes] * w_ref[j:j + 1, lanes]
            y_ref[r * CONV_ROWS:(r + 1) * CONV_ROWS, lanes] = acc
    y = y_ref[...] + bdw_ref[...]
    mu = jnp.mean(y, axis=-1, keepdims=True)
    d = y - mu
    var = jnp.mean(d * d, axis=-1, keepdims=True)
    z = d * lax.rsqrt(var + EPS) * gln_ref[...] + bln_ref[...]
    o_ref[0] = (z * _sigmoid(z)).astype(o_ref.dtype)


def _conv_ln(hglu, w_dw, b_dw, g_ln, b_ln):
    B, S, _ = hglu.shape
    per = CONV_TILE // CONV_HALO
    const = lambda shape: pl.BlockSpec(shape, lambda b, t: (0,) * len(shape))
    return pl.pallas_call(
        _conv_kernel,
        grid=(B, S // CONV_TILE),
        in_specs=[
            pl.BlockSpec((1, CONV_TILE, CONV_WIDTH), lambda b, t: (b, t, 0)),
            pl.BlockSpec((1, CONV_HALO, CONV_WIDTH), lambda b, t: (b, jnp.maximum(t * per - 1, 0), 0)),
            const(w_dw.shape), const((1, CONV_WIDTH)), const((1, CONV_WIDTH)), const((1, CONV_WIDTH)),
        ],
        out_specs=pl.BlockSpec((1, CONV_TILE, CONV_WIDTH), lambda b, t: (b, t, 0)),
        out_shape=jax.ShapeDtypeStruct((B, S, CONV_WIDTH), jnp.bfloat16),
        scratch_shapes=[
            pltpu.VMEM((CONV_HALO + CONV_TILE, CONV_WIDTH), jnp.float32),
            pltpu.VMEM((CONV_TILE, CONV_WIDTH), jnp.float32),
        ],
        compiler_params=pltpu.CompilerParams(
            dimension_semantics=("parallel", "arbitrary"), vmem_limit_bytes=VMEM_LIMIT),
        name="conv_ln",
    )(hglu, hglu, w_dw, b_dw, g_ln, b_ln)


MLP_TILE = 512
FF_CHUNK = 1024


def _out_mlp_kernel(x_ref, aT_ref, cv_ref, woa_ref, woc_ref, g_ref, w1_ref, w2_ref, gf_ref, o_ref,
                    h_ref, *, final_norm):
    attn = aT_ref[0].T
    h = x_ref[...]
    h = h + jnp.dot(attn, woa_ref[...], preferred_element_type=jnp.float32)
    h = h + jnp.dot(cv_ref[...], woc_ref[...], preferred_element_type=jnp.float32)
    h_ref[...] = h
    hn = _rms(h, g_ref[...]).astype(jnp.bfloat16)

    def chunk(c, carry):
        ff = jnp.maximum(jnp.dot(hn, w1_ref[c], preferred_element_type=jnp.float32), 0.0)
        h_ref[...] += jnp.dot((ff * ff).astype(jnp.bfloat16), w2_ref[c], preferred_element_type=jnp.float32)
        return carry

    lax.fori_loop(0, D_FF // FF_CHUNK, chunk, 0)
    h = h_ref[...]
    o_ref[...] = _rms(h, gf_ref[...]) if final_norm else h


def _out_mlp(x2d, attnT, conv2d, woa, woc, g_mlp, w1, w2, g_final, final_norm):
    T = x2d.shape[0]
    S = attnT.shape[2]
    per = S // MLP_TILE
    const = lambda shape: pl.BlockSpec(shape, lambda t: (0,) * len(shape), pipeline_mode=pl.Buffered(1))
    return pl.pallas_call(
        functools.partial(_out_mlp_kernel, final_norm=final_norm),
        grid=(T // MLP_TILE,),
        in_specs=[
            pl.BlockSpec((MLP_TILE, D_MODEL), lambda t: (t, 0)),
            pl.BlockSpec((1, ATTN_WIDTH, MLP_TILE), lambda t: (t // per, 0, t % per)),
            pl.BlockSpec((MLP_TILE, CONV_WIDTH), lambda t: (t, 0)),
            const(woa.shape), const(woc.shape), const((1, D_MODEL)),
            const(w1.shape), const(w2.shape), const((1, D_MODEL)),
        ],
        out_specs=pl.BlockSpec((MLP_TILE, D_MODEL), lambda t: (t, 0)),
        out_shape=jax.ShapeDtypeStruct((T, D_MODEL), jnp.float32),
        scratch_shapes=[pltpu.VMEM((MLP_TILE, D_MODEL), jnp.float32)],
        compiler_params=pltpu.CompilerParams(
            dimension_semantics=("parallel",), vmem_limit_bytes=VMEM_LIMIT),
        name="out_mlp",
    )(x2d, attnT, conv2d, woa, woc, g_mlp, w1, w2, g_final)


def _rope_tables(S):
    inv_freq = ROPE_THETA ** (-jnp.arange(ROT_HALF, dtype=jnp.float32) * 2.0 / ROT_DIM)
    ang = jnp.arange(S, dtype=jnp.float32)[:, None] * inv_freq[None, :]
    cos, sin = jnp.cos(ang), jnp.sin(ang)
    zeros = jnp.zeros((S, K_LANES - ROT_DIM), jnp.float32)
    zero8 = jnp.zeros((S, ROT_HALF), jnp.float32)
    c = jnp.concatenate([cos, cos, jnp.ones_like(zeros)], axis=1)
    sa = jnp.concatenate([-sin, zero8, zeros], axis=1)
    sb = jnp.concatenate([zero8, sin, zeros], axis=1)
    return cos.T, sin.T, c, sa, sb


def _split_w_in(w):
    bf = jnp.bfloat16
    wq = w[:, :ATTN_WIDTH].T.reshape(N_HEADS, HEAD_DIM, D_MODEL) * (HEAD_DIM ** -0.5)
    wq = jnp.pad(wq, ((0, 0), (0, Q_ROWS - HEAD_DIM), (0, 0))).reshape(N_HEADS * Q_ROWS, D_MODEL)
    wk = w[:, ATTN_WIDTH:2 * ATTN_WIDTH].reshape(D_MODEL, N_HEADS, HEAD_DIM)
    wk = jnp.pad(wk, ((0, 0), (0, 0), (0, K_LANES - HEAD_DIM))).reshape(D_MODEL, N_HEADS * K_LANES)
    wv = w[:, 2 * ATTN_WIDTH:3 * ATTN_WIDTH].T.reshape(N_HEADS, HEAD_DIM, D_MODEL)
    wv = jnp.pad(wv, ((0, 0), (0, V_ROWS - HEAD_DIM), (0, 0))).reshape(N_HEADS * V_ROWS, D_MODEL)
    wu = w[:, 3 * ATTN_WIDTH:]
    return wq.astype(bf), wk.astype(bf), wv.astype(bf), wu.astype(bf)


def kernel(x, g_mix_norm, w_in, b_glu, w_dw, b_dw, g_conv_ln, b_conv_ln, w_out, g_mlp_norm,
           w_mlp_in, w_mlp_out, g_final):
    B, S, D = x.shape
    depth = w_in.shape[0]
    bf = jnp.bfloat16
    tables = _rope_tables(S)
    h = x
    for l in range(depth):
        wq, wk, wv, wu = _split_w_in(w_in[l])
        qT, k, vT, kmean, hglu = _in_proj(
            h, g_mix_norm[l][None], wq, wk, wv, wu, b_glu[l][None], *tables)
        attnT = _moba_attention(qT, k, vT, kmean)
        conv = _conv_ln(hglu, w_dw[l][:, 0, :], b_dw[l][None], g_conv_ln[l][None], b_conv_ln[l][None])
        last = l == depth - 1
        h = _out_mlp(
            h.reshape(B * S, D), attnT, conv.reshape(B * S, CONV_WIDTH),
            w_out[l][:ATTN_WIDTH].astype(bf), w_out[l][ATTN_WIDTH:].astype(bf), g_mlp_norm[l][None],
            w_mlp_in[l].reshape(D, D_FF // FF_CHUNK, FF_CHUNK).transpose(1, 0, 2).astype(bf),
            w_mlp_out[l].reshape(D_FF // FF_CHUNK, FF_CHUNK, D).astype(bf), g_final[None], final_norm=last,
        ).reshape(B, S, D)
    return h
```

```python
import functools

import jax
import jax.numpy as jnp
from jax import lax
from jax.experimental import pallas as pl
from jax.experimental.pallas import tpu as pltpu

D_MODEL = 1024
ATTN_WIDTH = 512
CONV_WIDTH = 512
HEAD_DIM = 64
N_HEADS = 8
ROT_DIM = 16
ROT_HALF = 8
ROPE_THETA = 500000.0
MOBA_BLOCK = 256
MOBA_TOP_K = 3
CONV_K = 31
D_FF = 4096
EPS = 1e-6

Q_ROWS = 128
K_LANES = 128
BIAS_ROW0 = HEAD_DIM
V_ROWS = 80
MASKED = -1e30
LOG2E = 1.4426950408889634

VMEM_LIMIT = 56 * 1024 * 1024

_NT = (((1,), (1,)), ((), ()))


def _rms(x, g):
    return x * lax.rsqrt(jnp.mean(x * x, axis=-1, keepdims=True) + EPS) * g


def _sigmoid(x):
    return 1.0 / (1.0 + jnp.exp(-x))


CONV_HALO = 32
CONV_ROWS = 64
LANES = 128
SUBLANES = 8


def _in_proj_kernel(x_ref, g_ref, wq_ref, wk_ref, wv_ref, wu_ref, bglu_ref,
                    cosT_ref, sinT_ref, c_ref, sa_ref, sb_ref, wdw_ref, bdw_ref, gln_ref, bln_ref,
                    qT_ref, k_ref, vT_ref, kmean_ref, conv_ref, win_ref, y_ref):
    n = pl.program_id(1)

    @pl.when(n == 0)
    def _():
        win_ref[0, 0:CONV_HALO, :] = jnp.zeros((CONV_HALO, CONV_WIDTH), jnp.float32)

    xb = _rms(x_ref[0], g_ref[...]).astype(jnp.bfloat16)

    u = jnp.dot(xb, wu_ref[...], preferred_element_type=jnp.float32) + bglu_ref[...]
    win_ref[0, CONV_HALO:, :] = u[:, :CONV_WIDTH] * _sigmoid(u[:, CONV_WIDTH:])

    qT = lax.dot_general(wq_ref[...], xb, _NT, preferred_element_type=jnp.float32)
    q3 = qT.reshape(N_HEADS, Q_ROWS, MOBA_BLOCK)
    cos = cosT_ref[...][None]
    sin = sinT_ref[...][None]
    x1 = q3[:, 0:ROT_HALF]
    x2 = q3[:, ROT_HALF:ROT_DIM]
    q3 = jnp.concatenate([x1 * cos - x2 * sin, x2 * cos + x1 * sin, q3[:, ROT_DIM:]], axis=1)
    qT_ref[0] = (q3 * LOG2E).reshape(N_HEADS * Q_ROWS, MOBA_BLOCK).astype(jnp.bfloat16)

    k = jnp.dot(xb, wk_ref[...], preferred_element_type=jnp.float32)
    c = c_ref[...]
    sa = sa_ref[...]
    sb = sb_ref[...]
    parts = []
    for h in range(N_HEADS):
        kh = k[:, h * K_LANES:(h + 1) * K_LANES]
        parts.append(kh * c + pltpu.roll(kh, K_LANES - ROT_HALF, 1) * sa + pltpu.roll(kh, ROT_HALF, 1) * sb)
    k = jnp.concatenate(parts, axis=1)
    kmean_ref[0, pl.ds(n, 1), :] = jnp.mean(k, axis=0, keepdims=True)
    lane = lax.broadcasted_iota(jnp.int32, k.shape, 1) % K_LANES
    k_ref[0, 0] = jnp.where(lane == BIAS_ROW0 + n, 1.0, k).astype(jnp.bfloat16)

    vT = lax.dot_general(wv_ref[...], xb, _NT, preferred_element_type=jnp.float32)
    row = lax.broadcasted_iota(jnp.int32, vT.shape, 0) % V_ROWS
    vT_ref[0, 0] = jnp.where(row == HEAD_DIM, 1.0, vT).astype(jnp.bfloat16)

    rows = CONV_HALO + MOBA_BLOCK
    for b in range(1, SUBLANES):
        win_ref[b, 0:rows - SUBLANES, :] = win_ref[0, pl.ds(b, rows - SUBLANES), :]
    off = CONV_HALO - (CONV_K - 1)
    for ch in range(CONV_WIDTH // LANES):
        lanes = slice(ch * LANES, (ch + 1) * LANES)
        for r in range(MOBA_BLOCK // CONV_ROWS):
            acc = jnp.zeros((CONV_ROWS, LANES), jnp.float32)
            for j in range(CONV_K):
                a, b = divmod(off + j, SUBLANES)
                start = r * CONV_ROWS + a * SUBLANES
                acc = acc + win_ref[b, start:start + CONV_ROWS, lanes] * wdw_ref[j:j + 1, lanes]
            y_ref[r * CONV_ROWS:(r + 1) * CONV_ROWS, lanes] = acc
    win_ref[0, 0:CONV_HALO, :] = win_ref[0, MOBA_BLOCK:rows, :]

    y = y_ref[...] + bdw_ref[...]
    mu = jnp.mean(y, axis=-1, keepdims=True)
    d = y - mu
    var = jnp.mean(d * d, axis=-1, keepdims=True)
    z = d * lax.rsqrt(var + EPS) * gln_ref[...] + bln_ref[...]
    conv_ref[0] = (z * _sigmoid(z)).astype(conv_ref.dtype)


def _in_proj(x, g, wq, wk, wv, wu, bglu, cosT, sinT, c, sa, sb, w_dw, b_dw, g_ln, b_ln):
    B, S, _ = x.shape
    nb = S // MOBA_BLOCK
    const = lambda shape: pl.BlockSpec(shape, lambda b, n: (0,) * len(shape))
    row = const((1, CONV_WIDTH))
    return pl.pallas_call(
        _in_proj_kernel,
        grid=(B, nb),
        in_specs=[
            pl.BlockSpec((1, MOBA_BLOCK, D_MODEL), lambda b, n: (b, n, 0)),
            const((1, D_MODEL)),
            const(wq.shape), const(wk.shape), const(wv.shape), const(wu.shape),
            const((1, 2 * CONV_WIDTH)),
            pl.BlockSpec((ROT_HALF, MOBA_BLOCK), lambda b, n: (0, n)),
            pl.BlockSpec((ROT_HALF, MOBA_BLOCK), lambda b, n: (0, n)),
            pl.BlockSpec((MOBA_BLOCK, K_LANES), lambda b, n: (n, 0)),
            pl.BlockSpec((MOBA_BLOCK, K_LANES), lambda b, n: (n, 0)),
            pl.BlockSpec((MOBA_BLOCK, K_LANES), lambda b, n: (n, 0)),
            const(w_dw.shape), row, row, row,
        ],
        out_specs=[
            pl.BlockSpec((1, N_HEADS * Q_ROWS, MOBA_BLOCK), lambda b, n: (b, 0, n)),
            pl.BlockSpec((1, 1, MOBA_BLOCK, N_HEADS * K_LANES), lambda b, n: (b, n, 0, 0)),
            pl.BlockSpec((1, 1, N_HEADS * V_ROWS, MOBA_BLOCK), lambda b, n: (b, n, 0, 0)),
            pl.BlockSpec((1, nb, N_HEADS * K_LANES), lambda b, n: (b, 0, 0)),
            pl.BlockSpec((1, MOBA_BLOCK, CONV_WIDTH), lambda b, n: (b, n, 0)),
        ],
        out_shape=[
            jax.ShapeDtypeStruct((B, N_HEADS * Q_ROWS, S), jnp.bfloat16),
            jax.ShapeDtypeStruct((B, nb, MOBA_BLOCK, N_HEADS * K_LANES), jnp.bfloat16),
            jax.ShapeDtypeStruct((B, nb, N_HEADS * V_ROWS, MOBA_BLOCK), jnp.bfloat16),
            jax.ShapeDtypeStruct((B, nb, N_HEADS * K_LANES), jnp.float32),
            jax.ShapeDtypeStruct((B, S, CONV_WIDTH), jnp.bfloat16),
        ],
        scratch_shapes=[
            pltpu.VMEM((SUBLANES, CONV_HALO + MOBA_BLOCK, CONV_WIDTH), jnp.float32),
            pltpu.VMEM((MOBA_BLOCK, CONV_WIDTH), jnp.float32),
        ],
        compiler_params=pltpu.CompilerParams(
            dimension_semantics=("parallel", "arbitrary"), vmem_limit_bytes=VMEM_LIMIT),
        name="in_proj",
    )(x, g, wq, wk, wv, wu, bglu, cosT, sinT, c, sa, sb, w_dw, b_dw, g_ln, b_ln)


HEADS_PER_STEP = 4


def _moba_kernel(qT_ref, k_ref, vT_ref, kmean_ref, o_ref, s_ref, p_ref):
    i = pl.program_id(2)
    nb = k_ref.shape[1]
    kpos = lax.broadcasted_iota(jnp.int32, (MOBA_BLOCK, MOBA_BLOCK), 0)
    qpos = lax.broadcasted_iota(jnp.int32, (MOBA_BLOCK, MOBA_BLOCK), 1)
    causal = kpos <= qpos
    blk = lax.broadcasted_iota(jnp.int32, (nb, MOBA_BLOCK), 0)
    pad = jnp.zeros((Q_ROWS - HEAD_DIM - nb, MOBA_BLOCK), jnp.bfloat16)

    def stage_logits(h, n, qa):
        k_blk = k_ref[0, n, :, h * K_LANES:(h + 1) * K_LANES]
        return jnp.dot(k_blk, qa, preferred_element_type=jnp.float32)

    def stage_values(h, n, p):
        vT_blk = vT_ref[0, n, h * V_ROWS:(h + 1) * V_ROWS, :]
        return jnp.dot(vT_blk, p, preferred_element_type=jnp.float32)

    def stage_softmax(s, m, own):
        if own:
            s = jnp.where(causal, s, -jnp.inf)
        m_new = jnp.maximum(m, jnp.max(s, axis=0, keepdims=True))
        return m_new, jnp.exp2(m - m_new), jnp.exp2(s - m_new).astype(jnp.bfloat16)

    qas = []
    for h in range(HEADS_PER_STEP):
        qT = qT_ref[0, h * Q_ROWS:(h + 1) * Q_ROWS, :]

        km = kmean_ref[0, :, h * K_LANES:(h + 1) * K_LANES]
        km_hi = km.astype(jnp.bfloat16)
        km_lo = (km - km_hi.astype(jnp.float32)).astype(jnp.bfloat16)
        g2 = jnp.dot(jnp.concatenate([km_hi, km_lo], axis=0), qT, preferred_element_type=jnp.float32)
        gate = g2[:nb] + g2[nb:]

        rem = jnp.where(blk < i, gate, -jnp.inf)
        sel = blk == i
        for _ in range(MOBA_TOP_K):
            mx = jnp.max(rem, axis=0, keepdims=True)
            first = jnp.min(jnp.where(rem == mx, blk, nb), axis=0, keepdims=True)
            pick = (blk == first) & (mx > -jnp.inf)
            sel = sel | pick
            rem = jnp.where(pick, -jnp.inf, rem)
        bias = jnp.where(sel, 0.0, MASKED).astype(jnp.bfloat16)

        qa = jnp.concatenate([qT[:HEAD_DIM], bias, pad], axis=0)
        qas.append(qa)

    heads = range(HEADS_PER_STEP)
    for h in heads:
        s_ref[h] = stage_logits(h, 0, qas[h])
        p_ref[h] = jnp.zeros((MOBA_BLOCK, MOBA_BLOCK), jnp.bfloat16)

    def values_behind(t, carry):
        pvs = [stage_values(h, jnp.maximum(t - 1, 0), p_ref[h]) for h in heads]
        return [acc * alpha + pv for (_, alpha, acc), pv in zip(carry, pvs)]

    def body(t, carry):
        ahead = [stage_logits(h, t + 1, qas[h]) for h in heads]
        accs = values_behind(t, carry)
        new = []
        for h in heads:
            m_new, alpha, p = stage_softmax(s_ref[h], carry[h][0], own=False)
            p_ref[h] = p
            new.append((m_new, alpha, accs[h]))
        for h in heads:
            s_ref[h] = ahead[h]
        return tuple(new)

    init = (jnp.full((1, MOBA_BLOCK), -jnp.inf, jnp.float32), jnp.ones((1, MOBA_BLOCK), jnp.float32),
            jnp.zeros((V_ROWS, MOBA_BLOCK), jnp.float32))
    carry = lax.fori_loop(0, i, body, (init,) * HEADS_PER_STEP)
    accs = values_behind(i, carry)
    for h in heads:
        _, alpha, p = stage_softmax(s_ref[h], carry[h][0], own=True)
        acc = accs[h] * alpha + stage_values(h, i, p)
        o_ref[0, h * HEAD_DIM:(h + 1) * HEAD_DIM, :] = (
            acc[:HEAD_DIM] * (1.0 / acc[HEAD_DIM:HEAD_DIM + 1])).astype(o_ref.dtype)


def _moba_attention(qT, k, vT, kmean):
    B, _, S = qT.shape
    nb = S // MOBA_BLOCK
    hp = HEADS_PER_STEP
    return pl.pallas_call(
        _moba_kernel,
        grid=(B, N_HEADS // hp, nb),
        in_specs=[
            pl.BlockSpec((1, hp * Q_ROWS, MOBA_BLOCK), lambda b, g, i: (b, g, i)),
            pl.BlockSpec((1, nb, MOBA_BLOCK, hp * K_LANES), lambda b, g, i: (b, 0, 0, g)),
            pl.BlockSpec((1, nb, hp * V_ROWS, MOBA_BLOCK), lambda b, g, i: (b, 0, g, 0)),
            pl.BlockSpec((1, nb, hp * K_LANES), lambda b, g, i: (b, 0, g)),
        ],
        out_specs=pl.BlockSpec((1, hp * HEAD_DIM, MOBA_BLOCK), lambda b, g, i: (b, g, i)),
        out_shape=jax.ShapeDtypeStruct((B, ATTN_WIDTH, S), jnp.bfloat16),
        scratch_shapes=[
            pltpu.VMEM((hp, MOBA_BLOCK, MOBA_BLOCK), jnp.float32),
            pltpu.VMEM((hp, MOBA_BLOCK, MOBA_BLOCK), jnp.bfloat16),
        ],
        compiler_params=pltpu.CompilerParams(
            dimension_semantics=("parallel", "parallel", "arbitrary"), vmem_limit_bytes=VMEM_LIMIT),
        name="moba_attn",
    )(qT, k, vT, kmean)


MLP_TILE = 512
FF_CHUNK = 1024


def _out_mlp_kernel(x_ref, aT_ref, cv_ref, woa_ref, woc_ref, g_ref, w1_ref, w2_ref, gf_ref, o_ref,
                    h_ref, *, final_norm):
    attn = aT_ref[0].T
    h = x_ref[...]
    h = h + jnp.dot(attn, woa_ref[...], preferred_element_type=jnp.float32)
    h = h + jnp.dot(cv_ref[...], woc_ref[...], preferred_element_type=jnp.float32)
    h_ref[...] = h
    hn = _rms(h, g_ref[...]).astype(jnp.bfloat16)

    def chunk(c, carry):
        ff = jnp.maximum(jnp.dot(hn, w1_ref[c], preferred_element_type=jnp.float32), 0.0)
        h_ref[...] += jnp.dot((ff * ff).astype(jnp.bfloat16), w2_ref[c], preferred_element_type=jnp.float32)
        return carry

    lax.fori_loop(0, D_FF // FF_CHUNK, chunk, 0)
    h = h_ref[...]
    o_ref[...] = _rms(h, gf_ref[...]) if final_norm else h


def _out_mlp(x2d, attnT, conv2d, woa, woc, g_mlp, w1, w2, g_final, final_norm):
    T = x2d.shape[0]
    S = attnT.shape[2]
    per = S // MLP_TILE
    const = lambda shape: pl.BlockSpec(shape, lambda t: (0,) * len(shape), pipeline_mode=pl.Buffered(1))
    return pl.pallas_call(
        functools.partial(_out_mlp_kernel, final_norm=final_norm),
        grid=(T // MLP_TILE,),
        in_specs=[
            pl.BlockSpec((MLP_TILE, D_MODEL), lambda t: (t, 0)),
            pl.BlockSpec((1, ATTN_WIDTH, MLP_TILE), lambda t: (t // per, 0, t % per)),
            pl.BlockSpec((MLP_TILE, CONV_WIDTH), lambda t: (t, 0)),
            const(woa.shape), const(woc.shape), const((1, D_MODEL)),
            const(w1.shape), const(w2.shape), const((1, D_MODEL)),
        ],
        out_specs=pl.BlockSpec((MLP_TILE, D_MODEL), lambda t: (t, 0)),
        out_shape=jax.ShapeDtypeStruct((T, D_MODEL), jnp.float32),
        scratch_shapes=[pltpu.VMEM((MLP_TILE, D_MODEL), jnp.float32)],
        compiler_params=pltpu.CompilerParams(
            dimension_semantics=("parallel",), vmem_limit_bytes=VMEM_LIMIT),
        name="out_mlp",
    )(x2d, attnT, conv2d, woa, woc, g_mlp, w1, w2, g_final)


def _rope_tables(S):
    inv_freq = ROPE_THETA ** (-jnp.arange(ROT_HALF, dtype=jnp.float32) * 2.0 / ROT_DIM)
    ang = jnp.arange(S, dtype=jnp.float32)[:, None] * inv_freq[None, :]
    cos, sin = jnp.cos(ang), jnp.sin(ang)
    zeros = jnp.zeros((S, K_LANES - ROT_DIM), jnp.float32)
    zero8 = jnp.zeros((S, ROT_HALF), jnp.float32)
    c = jnp.concatenate([cos, cos, jnp.ones_like(zeros)], axis=1)
    sa = jnp.concatenate([-sin, zero8, zeros], axis=1)
    sb = jnp.concatenate([zero8, sin, zeros], axis=1)
    return cos.T, sin.T, c, sa, sb


def _split_w_in(w):
    bf = jnp.bfloat16
    wq = w[:, :ATTN_WIDTH].T.reshape(N_HEADS, HEAD_DIM, D_MODEL) * (HEAD_DIM ** -0.5)
    wq = jnp.pad(wq, ((0, 0), (0, Q_ROWS - HEAD_DIM), (0, 0))).reshape(N_HEADS * Q_ROWS, D_MODEL)
    wk = w[:, ATTN_WIDTH:2 * ATTN_WIDTH].reshape(D_MODEL, N_HEADS, HEAD_DIM)
    wk = jnp.pad(wk, ((0, 0), (0, 0), (0, K_LANES - HEAD_DIM))).reshape(D_MODEL, N_HEADS * K_LANES)
    wv = w[:, 2 * ATTN_WIDTH:3 * ATTN_WIDTH].T.reshape(N_HEADS, HEAD_DIM, D_MODEL)
    wv = jnp.pad(wv, ((0, 0), (0, V_ROWS - HEAD_DIM), (0, 0))).reshape(N_HEADS * V_ROWS, D_MODEL)
    wu = w[:, 3 * ATTN_WIDTH:]
    return wq.astype(bf), wk.astype(bf), wv.astype(bf), wu.astype(bf)


def kernel(x, g_mix_norm, w_in, b_glu, w_dw, b_dw, g_conv_ln, b_conv_ln, w_out, g_mlp_norm,
           w_mlp_in, w_mlp_out, g_final):
    B, S, D = x.shape
    depth = w_in.shape[0]
    bf = jnp.bfloat16
    tables = _rope_tables(S)
    h = x
    for l in range(depth):
        wq, wk, wv, wu = _split_w_in(w_in[l])
        qT, k, vT, kmean, conv = _in_proj(
            h, g_mix_norm[l][None], wq, wk, wv, wu, b_glu[l][None], *tables,
            w_dw[l][:, 0, :], b_dw[l][None], g_conv_ln[l][None], b_conv_ln[l][None])
        attnT = _moba_attention(qT, k, vT, kmean)
        last = l == depth - 1
        h = _out_mlp(
            h.reshape(B * S, D), attnT, conv.reshape(B * S, CONV_WIDTH),
            w_out[l][:ATTN_WIDTH].astype(bf), w_out[l][ATTN_WIDTH:].astype(bf), g_mlp_norm[l][None],
            w_mlp_in[l].reshape(D, D_FF // FF_CHUNK, FF_CHUNK).transpose(1, 0, 2).astype(bf),
            w_mlp_out[l].reshape(D_FF // FF_CHUNK, FF_CHUNK, D).astype(bf), g_final[None], final_norm=last,
        ).reshape(B, S, D)
    return h
```

```python
import functools

import jax
import jax.numpy as jnp
from jax import lax
from jax.experimental import pallas as pl
from jax.experimental.pallas import tpu as pltpu

D_MODEL = 1024
ATTN_WIDTH = 512
CONV_WIDTH = 512
HEAD_DIM = 64
N_HEADS = 8
ROT_DIM = 16
ROT_HALF = 8
ROPE_THETA = 500000.0
MOBA_BLOCK = 256
MOBA_TOP_K = 3
CONV_K = 31
D_FF = 4096
EPS = 1e-6

Q_ROWS = 128
K_LANES = 128
BIAS_ROW0 = HEAD_DIM
V_ROWS = 80
MASKED = -1e30
LOG2E = 1.4426950408889634

VMEM_LIMIT = 56 * 1024 * 1024

_NT = (((1,), (1,)), ((), ()))


def _rms(x, g):
    return x * lax.rsqrt(jnp.mean(x * x, axis=-1, keepdims=True) + EPS) * g


def _sigmoid(x):
    return 1.0 / (1.0 + jnp.exp(-x))


CONV_HALO = 32
CONV_ROWS = 64
LANES = 128
SUBLANES = 8


def _in_proj_kernel(x_ref, g_ref, wq_ref, wk_ref, wv_ref, wu_ref, bglu_ref,
                    cosT_ref, sinT_ref, c_ref, sa_ref, sb_ref, wdw_ref, bdw_ref, gln_ref, bln_ref,
                    qT_ref, k_ref, vT_ref, kmean_ref, conv_ref, win_ref, y_ref):
    n = pl.program_id(1)

    @pl.when(n == 0)
    def _():
        win_ref[0, 0:CONV_HALO, :] = jnp.zeros((CONV_HALO, CONV_WIDTH), jnp.float32)

    xb = _rms(x_ref[0], g_ref[...]).astype(jnp.bfloat16)

    u = jnp.dot(xb, wu_ref[...], preferred_element_type=jnp.float32) + bglu_ref[...]
    win_ref[0, CONV_HALO:, :] = u[:, :CONV_WIDTH] * _sigmoid(u[:, CONV_WIDTH:])

    qT = lax.dot_general(wq_ref[...], xb, _NT, preferred_element_type=jnp.float32)
    q3 = qT.reshape(N_HEADS, Q_ROWS, MOBA_BLOCK)
    cos = cosT_ref[...][None]
    sin = sinT_ref[...][None]
    x1 = q3[:, 0:ROT_HALF]
    x2 = q3[:, ROT_HALF:ROT_DIM]
    q3 = jnp.concatenate([x1 * cos - x2 * sin, x2 * cos + x1 * sin, q3[:, ROT_DIM:]], axis=1)
    qT_ref[0] = (q3 * LOG2E).reshape(N_HEADS * Q_ROWS, MOBA_BLOCK).astype(jnp.bfloat16)

    k = jnp.dot(xb, wk_ref[...], preferred_element_type=jnp.float32)
    c = c_ref[...]
    sa = sa_ref[...]
    sb = sb_ref[...]
    parts = []
    for h in range(N_HEADS):
        kh = k[:, h * K_LANES:(h + 1) * K_LANES]
        parts.append(kh * c + pltpu.roll(kh, K_LANES - ROT_HALF, 1) * sa + pltpu.roll(kh, ROT_HALF, 1) * sb)
    k = jnp.concatenate(parts, axis=1)
    kmean_ref[0, pl.ds(n, 1), :] = jnp.mean(k, axis=0, keepdims=True)
    lane = lax.broadcasted_iota(jnp.int32, k.shape, 1) % K_LANES
    k_ref[0, 0] = jnp.where(lane == BIAS_ROW0 + n, 1.0, k).astype(jnp.bfloat16)

    vT = lax.dot_general(wv_ref[...], xb, _NT, preferred_element_type=jnp.float32)
    row = lax.broadcasted_iota(jnp.int32, vT.shape, 0) % V_ROWS
    vT_ref[0, 0] = jnp.where(row == HEAD_DIM, 1.0, vT).astype(jnp.bfloat16)

    rows = CONV_HALO + MOBA_BLOCK
    for b in range(1, SUBLANES):
        win_ref[b, 0:rows - SUBLANES, :] = win_ref[0, pl.ds(b, rows - SUBLANES), :]
    off = CONV_HALO - (CONV_K - 1)
    for ch in range(CONV_WIDTH // LANES):
        lanes = slice(ch * LANES, (ch + 1) * LANES)
        for r in range(MOBA_BLOCK // CONV_ROWS):
            acc = jnp.zeros((CONV_ROWS, LANES), jnp.float32)
            for j in range(CONV_K):
                a, b = divmod(off + j, SUBLANES)
                start = r * CONV_ROWS + a * SUBLANES
                acc = acc + win_ref[b, start:start + CONV_ROWS, lanes] * wdw_ref[j:j + 1, lanes]
            y_ref[r * CONV_ROWS:(r + 1) * CONV_ROWS, lanes] = acc
    win_ref[0, 0:CONV_HALO, :] = win_ref[0, MOBA_BLOCK:rows, :]

    y = y_ref[...] + bdw_ref[...]
    mu = jnp.mean(y, axis=-1, keepdims=True)
    d = y - mu
    var = jnp.mean(d * d, axis=-1, keepdims=True)
    z = d * lax.rsqrt(var + EPS) * gln_ref[...] + bln_ref[...]
    conv_ref[0] = (z * _sigmoid(z)).astype(conv_ref.dtype)


def _in_proj(x, g, wq, wk, wv, wu, bglu, cosT, sinT, c, sa, sb, w_dw, b_dw, g_ln, b_ln):
    B, S, _ = x.shape
    nb = S // MOBA_BLOCK
    const = lambda shape: pl.BlockSpec(shape, lambda b, n: (0,) * len(shape))
    row = const((1, CONV_WIDTH))
    return pl.pallas_call(
        _in_proj_kernel,
        grid=(B, nb),
        in_specs=[
            pl.BlockSpec((1, MOBA_BLOCK, D_MODEL), lambda b, n: (b, n, 0)),
            const((1, D_MODEL)),
            const(wq.shape), const(wk.shape), const(wv.shape), const(wu.shape),
            const((1, 2 * CONV_WIDTH)),
            pl.BlockSpec((ROT_HALF, MOBA_BLOCK), lambda b, n: (0, n)),
            pl.BlockSpec((ROT_HALF, MOBA_BLOCK), lambda b, n: (0, n)),
            pl.BlockSpec((MOBA_BLOCK, K_LANES), lambda b, n: (n, 0)),
            pl.BlockSpec((MOBA_BLOCK, K_LANES), lambda b, n: (n, 0)),
            pl.BlockSpec((MOBA_BLOCK, K_LANES), lambda b, n: (n, 0)),
            const(w_dw.shape), row, row, row,
        ],
        out_specs=[
            pl.BlockSpec((1, N_HEADS * Q_ROWS, MOBA_BLOCK), lambda b, n: (b, 0, n)),
            pl.BlockSpec((1, 1, MOBA_BLOCK, N_HEADS * K_LANES), lambda b, n: (b, n, 0, 0)),
            pl.BlockSpec((1, 1, N_HEADS * V_ROWS, MOBA_BLOCK), lambda b, n: (b, n, 0, 0)),
            pl.BlockSpec((1, nb, N_HEADS * K_LANES), lambda b, n: (b, 0, 0)),
            pl.BlockSpec((1, MOBA_BLOCK, CONV_WIDTH), lambda b, n: (b, n, 0)),
        ],
        out_shape=[
            jax.ShapeDtypeStruct((B, N_HEADS * Q_ROWS, S), jnp.bfloat16),
            jax.ShapeDtypeStruct((B, nb, MOBA_BLOCK, N_HEADS * K_LANES), jnp.bfloat16),
            jax.ShapeDtypeStruct((B, nb, N_HEADS * V_ROWS, MOBA_BLOCK), jnp.bfloat16),
            jax.ShapeDtypeStruct((B, nb, N_HEADS * K_LANES), jnp.float32),
            jax.ShapeDtypeStruct((B, S, CONV_WIDTH), jnp.bfloat16),
        ],
        scratch_shapes=[
            pltpu.VMEM((SUBLANES, CONV_HALO + MOBA_BLOCK, CONV_WIDTH), jnp.float32),
            pltpu.VMEM((MOBA_BLOCK, CONV_WIDTH), jnp.float32),
        ],
        compiler_params=pltpu.CompilerParams(
            dimension_semantics=("parallel", "arbitrary"), vmem_limit_bytes=VMEM_LIMIT),
        name="in_proj",
    )(x, g, wq, wk, wv, wu, bglu, cosT, sinT, c, sa, sb, w_dw, b_dw, g_ln, b_ln)


HEADS_PER_STEP = 4


def _moba_kernel(qT_ref, k_ref, vT_ref, kmean_ref, o_ref, s_ref, p_ref):
    i = pl.program_id(2)
    nb = k_ref.shape[1]
    kpos = lax.broadcasted_iota(jnp.int32, (MOBA_BLOCK, MOBA_BLOCK), 0)
    qpos = lax.broadcasted_iota(jnp.int32, (MOBA_BLOCK, MOBA_BLOCK), 1)
    causal = kpos <= qpos
    blk = lax.broadcasted_iota(jnp.int32, (nb, MOBA_BLOCK), 0)
    pad = jnp.zeros((Q_ROWS - HEAD_DIM - nb, MOBA_BLOCK), jnp.bfloat16)

    def stage_logits(h, n, qa):
        k_blk = k_ref[0, n, :, h * K_LANES:(h + 1) * K_LANES]
        return jnp.dot(k_blk, qa, preferred_element_type=jnp.float32)

    def stage_values(h, n, p):
        vT_blk = vT_ref[0, n, h * V_ROWS:(h + 1) * V_ROWS, :]
        return jnp.dot(vT_blk, p, preferred_element_type=jnp.float32)

    def stage_softmax(s, m, own):
        if own:
            s = jnp.where(causal, s, -jnp.inf)
        m_new = jnp.maximum(m, jnp.max(s, axis=0, keepdims=True))
        return m_new, jnp.exp2(m - m_new), jnp.exp2(s - m_new).astype(jnp.bfloat16)

    qas = []
    for h in range(HEADS_PER_STEP):
        qT = qT_ref[0, h * Q_ROWS:(h + 1) * Q_ROWS, :]

        km = kmean_ref[0, :, h * K_LANES:(h + 1) * K_LANES]
        km_hi = km.astype(jnp.bfloat16)
        km_lo = (km - km_hi.astype(jnp.float32)).astype(jnp.bfloat16)
        g2 = jnp.dot(jnp.concatenate([km_hi, km_lo], axis=0), qT, preferred_element_type=jnp.float32)
        gate = g2[:nb] + g2[nb:]

        rem = jnp.where(blk < i, gate, -jnp.inf)
        sel = blk == i
        for _ in range(MOBA_TOP_K):
            mx = jnp.max(rem, axis=0, keepdims=True)
            first = jnp.min(jnp.where(rem == mx, blk, nb), axis=0, keepdims=True)
            pick = (blk == first) & (mx > -jnp.inf)
            sel = sel | pick
            rem = jnp.where(pick, -jnp.inf, rem)
        bias = jnp.where(sel, 0.0, MASKED).astype(jnp.bfloat16)

        qa = jnp.concatenate([qT[:HEAD_DIM], bias, pad], axis=0)
        qas.append(qa)

    heads = range(HEADS_PER_STEP)
    for h in heads:
        s_ref[h] = stage_logits(h, 0, qas[h])
        p_ref[h] = jnp.zeros((MOBA_BLOCK, MOBA_BLOCK), jnp.bfloat16)

    def values_behind(t, carry):
        pvs = [stage_values(h, jnp.maximum(t - 1, 0), p_ref[h]) for h in heads]
        return [acc * alpha + pv for (_, alpha, acc), pv in zip(carry, pvs)]

    def body(t, carry):
        ahead = [stage_logits(h, t + 1, qas[h]) for h in heads]
        accs = values_behind(t, carry)
        new = []
        for h in heads:
            m_new, alpha, p = stage_softmax(s_ref[h], carry[h][0], own=False)
            p_ref[h] = p
            new.append((m_new, alpha, accs[h]))
        for h in heads:
            s_ref[h] = ahead[h]
        return tuple(new)

    init = (jnp.full((1, MOBA_BLOCK), -jnp.inf, jnp.float32), jnp.ones((1, MOBA_BLOCK), jnp.float32),
            jnp.zeros((V_ROWS, MOBA_BLOCK), jnp.float32))
    carry = lax.fori_loop(0, i // 2, lambda tt, c: body(2 * tt + 1, body(2 * tt, c)),
                          (init,) * HEADS_PER_STEP)
    carry = lax.cond(i % 2 == 1, lambda c: body(i - 1, c), lambda c: c, carry)
    accs = values_behind(i, carry)
    for h in heads:
        _, alpha, p = stage_softmax(s_ref[h], carry[h][0], own=True)
        acc = accs[h] * alpha + stage_values(h, i, p)
        o_ref[0, h * HEAD_DIM:(h + 1) * HEAD_DIM, :] = (
            acc[:HEAD_DIM] * (1.0 / acc[HEAD_DIM:HEAD_DIM + 1])).astype(o_ref.dtype)


def _moba_attention(qT, k, vT, kmean):
    B, _, S = qT.shape
    nb = S // MOBA_BLOCK
    hp = HEADS_PER_STEP
    return pl.pallas_call(
        _moba_kernel,
        grid=(B, N_HEADS // hp, nb),
        in_specs=[
            pl.BlockSpec((1, hp * Q_ROWS, MOBA_BLOCK), lambda b, g, i: (b, g, i)),
            pl.BlockSpec((1, nb, MOBA_BLOCK, hp * K_LANES), lambda b, g, i: (b, 0, 0, g)),
            pl.BlockSpec((1, nb, hp * V_ROWS, MOBA_BLOCK), lambda b, g, i: (b, 0, g, 0)),
            pl.BlockSpec((1, nb, hp * K_LANES), lambda b, g, i: (b, 0, g)),
        ],
        out_specs=pl.BlockSpec((1, hp * HEAD_DIM, MOBA_BLOCK), lambda b, g, i: (b, g, i)),
        out_shape=jax.ShapeDtypeStruct((B, ATTN_WIDTH, S), jnp.bfloat16),
        scratch_shapes=[
            pltpu.VMEM((hp, MOBA_BLOCK, MOBA_BLOCK), jnp.float32),
            pltpu.VMEM((hp, MOBA_BLOCK, MOBA_BLOCK), jnp.bfloat16),
        ],
        compiler_params=pltpu.CompilerParams(
            dimension_semantics=("parallel", "parallel", "arbitrary"), vmem_limit_bytes=VMEM_LIMIT),
        name="moba_attn",
    )(qT, k, vT, kmean)


MLP_TILE = 512
FF_CHUNK = 1024


def _out_mlp_kernel(x_ref, aT_ref, cv_ref, woa_ref, woc_ref, g_ref, w1_ref, w2_ref, gf_ref, o_ref,
                    h_ref, *, final_norm):
    attn = aT_ref[0].T
    h = x_ref[...]
    h = h + jnp.dot(attn, woa_ref[...], preferred_element_type=jnp.float32)
    h = h + jnp.dot(cv_ref[...], woc_ref[...], preferred_element_type=jnp.float32)
    h_ref[...] = h
    hn = _rms(h, g_ref[...]).astype(jnp.bfloat16)

    def chunk(c, carry):
        ff = jnp.maximum(jnp.dot(hn, w1_ref[c], preferred_element_type=jnp.float32), 0.0)
        h_ref[...] += jnp.dot((ff * ff).astype(jnp.bfloat16), w2_ref[c], preferred_element_type=jnp.float32)
        return carry

    lax.fori_loop(0, D_FF // FF_CHUNK, chunk, 0)
    h = h_ref[...]
    o_ref[...] = _rms(h, gf_ref[...]) if final_norm else h


def _out_mlp(x2d, attnT, conv2d, woa, woc, g_mlp, w1, w2, g_final, final_norm):
    T = x2d.shape[0]
    S = attnT.shape[2]
    per = S // MLP_TILE
    const = lambda shape: pl.BlockSpec(shape, lambda t: (0,) * len(shape), pipeline_mode=pl.Buffered(1))
    return pl.pallas_call(
        functools.partial(_out_mlp_kernel, final_norm=final_norm),
        grid=(T // MLP_TILE,),
        in_specs=[
            pl.BlockSpec((MLP_TILE, D_MODEL), lambda t: (t, 0)),
            pl.BlockSpec((1, ATTN_WIDTH, MLP_TILE), lambda t: (t // per, 0, t % per)),
            pl.BlockSpec((MLP_TILE, CONV_WIDTH), lambda t: (t, 0)),
            const(woa.shape), const(woc.shape), const((1, D_MODEL)),
            const(w1.shape), const(w2.shape), const((1, D_MODEL)),
        ],
        out_specs=pl.BlockSpec((MLP_TILE, D_MODEL), lambda t: (t, 0)),
        out_shape=jax.ShapeDtypeStruct((T, D_MODEL), jnp.float32),
        scratch_shapes=[pltpu.VMEM((MLP_TILE, D_MODEL), jnp.float32)],
        compiler_params=pltpu.CompilerParams(
            dimension_semantics=("parallel",), vmem_limit_bytes=VMEM_LIMIT),
        name="out_mlp",
    )(x2d, attnT, conv2d, woa, woc, g_mlp, w1, w2, g_final)


def _rope_tables(S):
    inv_freq = ROPE_THETA ** (-jnp.arange(ROT_HALF, dtype=jnp.float32) * 2.0 / ROT_DIM)
    ang = jnp.arange(S, dtype=jnp.float32)[:, None] * inv_freq[None, :]
    cos, sin = jnp.cos(ang), jnp.sin(ang)
    zeros = jnp.zeros((S, K_LANES - ROT_DIM), jnp.float32)
    zero8 = jnp.zeros((S, ROT_HALF), jnp.float32)
    c = jnp.concatenate([cos, cos, jnp.ones_like(zeros)], axis=1)
    sa = jnp.concatenate([-sin, zero8, zeros], axis=1)
    sb = jnp.concatenate([zero8, sin, zeros], axis=1)
    return cos.T, sin.T, c, sa, sb


def _split_w_in(w):
    bf = jnp.bfloat16
    wq = w[:, :ATTN_WIDTH].T.reshape(N_HEADS, HEAD_DIM, D_MODEL) * (HEAD_DIM ** -0.5)
    wq = jnp.pad(wq, ((0, 0), (0, Q_ROWS - HEAD_DIM), (0, 0))).reshape(N_HEADS * Q_ROWS, D_MODEL)
    wk = w[:, ATTN_WIDTH:2 * ATTN_WIDTH].reshape(D_MODEL, N_HEADS, HEAD_DIM)
    wk = jnp.pad(wk, ((0, 0), (0, 0), (0, K_LANES - HEAD_DIM))).reshape(D_MODEL, N_HEADS * K_LANES)
    wv = w[:, 2 * ATTN_WIDTH:3 * ATTN_WIDTH].T.reshape(N_HEADS, HEAD_DIM, D_MODEL)
    wv = jnp.pad(wv, ((0, 0), (0, V_ROWS - HEAD_DIM), (0, 0))).reshape(N_HEADS * V_ROWS, D_MODEL)
    wu = w[:, 3 * ATTN_WIDTH:]
    return wq.astype(bf), wk.astype(bf), wv.astype(bf), wu.astype(bf)


def kernel(x, g_mix_norm, w_in, b_glu, w_dw, b_dw, g_conv_ln, b_conv_ln, w_out, g_mlp_norm,
           w_mlp_in, w_mlp_out, g_final):
    B, S, D = x.shape
    depth = w_in.shape[0]
    bf = jnp.bfloat16
    tables = _rope_tables(S)
    h = x
    for l in range(depth):
        wq, wk, wv, wu = _split_w_in(w_in[l])
        qT, k, vT, kmean, conv = _in_proj(
            h, g_mix_norm[l][None], wq, wk, wv, wu, b_glu[l][None], *tables,
            w_dw[l][:, 0, :], b_dw[l][None], g_conv_ln[l][None], b_conv_ln[l][None])
        attnT = _moba_attention(qT, k, vT, kmean)
        last = l == depth - 1
        h = _out_mlp(
            h.reshape(B * S, D), attnT, conv.reshape(B * S, CONV_WIDTH),
            w_out[l][:ATTN_WIDTH].astype(bf), w_out[l][ATTN_WIDTH:].astype(bf), g_mlp_norm[l][None],
            w_mlp_in[l].reshape(D, D_FF // FF_CHUNK, FF_CHUNK).transpose(1, 0, 2).astype(bf),
            w_mlp_out[l].reshape(D_FF // FF_CHUNK, FF_CHUNK, D).astype(bf), g_final[None], final_norm=last,
        ).reshape(B, S, D)
    return h
```

```python
import functools

import jax
import jax.numpy as jnp
from jax import lax
from jax.experimental import pallas as pl
from jax.experimental.pallas import tpu as pltpu

D_MODEL = 1024
ATTN_WIDTH = 512
CONV_WIDTH = 512
HEAD_DIM = 64
N_HEADS = 8
ROT_DIM = 16
ROT_HALF = 8
ROPE_THETA = 500000.0
MOBA_BLOCK = 256
MOBA_TOP_K = 3
CONV_K = 31
D_FF = 4096
EPS = 1e-6

Q_ROWS = 128
K_LANES = 128
BIAS_ROW0 = HEAD_DIM
V_ROWS = 80
MASKED = -1e30
LOG2E = 1.4426950408889634

VMEM_LIMIT = 56 * 1024 * 1024

_NT = (((1,), (1,)), ((), ()))


def _rms(x, g):
    return x * lax.rsqrt(jnp.mean(x * x, axis=-1, keepdims=True) + EPS) * g


def _sigmoid(x):
    return 1.0 / (1.0 + jnp.exp(-x))


CONV_HALO = 32
CONV_ROWS = 64
LANES = 128
SUBLANES = 8


def _in_proj_kernel(x_ref, wq_ref, wk_ref, wv_ref, wu_ref, bglu_ref,
                    cosT_ref, sinT_ref, c_ref, sa_ref, sb_ref, wdw_ref, bdw_ref, gln_ref, bln_ref,
                    qT_ref, k_ref, vT_ref, kmean_ref, conv_ref, win_ref, y_ref):
    n = pl.program_id(1)

    @pl.when(n == 0)
    def _():
        win_ref[0, 0:CONV_HALO, :] = jnp.zeros((CONV_HALO, CONV_WIDTH), jnp.float32)

    x = x_ref[0]
    xb = (x * lax.rsqrt(jnp.mean(x * x, axis=-1, keepdims=True) + EPS)).astype(jnp.bfloat16)

    rows = CONV_HALO + MOBA_BLOCK
    off = CONV_HALO - (CONV_K - 1)
    for ch in range(CONV_WIDTH // LANES):
        lanes = slice(ch * LANES, (ch + 1) * LANES)
        u = lax.dot_general(xb, wu_ref[ch], _NT, preferred_element_type=jnp.float32) + bglu_ref[ch]
        win_ref[0, CONV_HALO:, lanes] = u[:, :LANES] * _sigmoid(u[:, LANES:])
        for b in range(1, SUBLANES):
            win_ref[b, 0:rows - SUBLANES, lanes] = win_ref[0, pl.ds(b, rows - SUBLANES), lanes]
        for r in range(MOBA_BLOCK // CONV_ROWS):
            acc = jnp.zeros((CONV_ROWS, LANES), jnp.float32)
            for j in range(CONV_K):
                a, b = divmod(off + j, SUBLANES)
                start = r * CONV_ROWS + a * SUBLANES
                acc = acc + win_ref[b, start:start + CONV_ROWS, lanes] * wdw_ref[j:j + 1, lanes]
            y_ref[r * CONV_ROWS:(r + 1) * CONV_ROWS, lanes] = acc
        win_ref[0, 0:CONV_HALO, lanes] = win_ref[0, MOBA_BLOCK:rows, lanes]

    qT = lax.dot_general(wq_ref[...], xb, _NT, preferred_element_type=jnp.float32)
    q3 = qT.reshape(N_HEADS, Q_ROWS, MOBA_BLOCK)
    cos = cosT_ref[...][None]
    sin = sinT_ref[...][None]
    x1 = q3[:, 0:ROT_HALF]
    x2 = q3[:, ROT_HALF:ROT_DIM]
    q3 = jnp.concatenate([x1 * cos - x2 * sin, x2 * cos + x1 * sin, q3[:, ROT_DIM:]], axis=1)
    qT_ref[0] = q3.reshape(N_HEADS * Q_ROWS, MOBA_BLOCK).astype(jnp.bfloat16)

    k = lax.dot_general(xb, wk_ref[...], _NT, preferred_element_type=jnp.float32)
    c = c_ref[...]
    sa = sa_ref[...]
    sb = sb_ref[...]
    parts = []
    for h in range(N_HEADS):
        kh = k[:, h * K_LANES:(h + 1) * K_LANES]
        parts.append(kh * c + pltpu.roll(kh, K_LANES - ROT_HALF, 1) * sa + pltpu.roll(kh, ROT_HALF, 1) * sb)
    k = jnp.concatenate(parts, axis=1)
    kmean_ref[0, pl.ds(n, 1), :] = jnp.mean(k, axis=0, keepdims=True)
    lane = lax.broadcasted_iota(jnp.int32, (1, N_HEADS * K_LANES), 1) % K_LANES
    k_ref[0, 0] = (k + jnp.where(lane == BIAS_ROW0 + n, 1.0, 0.0)).astype(jnp.bfloat16)

    vT = lax.dot_general(wv_ref[...], xb, _NT, preferred_element_type=jnp.float32)
    vT_ref[0, 0] = vT.astype(jnp.bfloat16)
    first = lax.broadcasted_iota(jnp.int32, (V_ROWS - HEAD_DIM, MOBA_BLOCK), 0) == 0
    ones_rows = jnp.where(first, 1.0, 0.0).astype(jnp.bfloat16)
    for h in range(N_HEADS):
        vT_ref[0, 0, h * V_ROWS + HEAD_DIM:(h + 1) * V_ROWS, :] = ones_rows

    y = y_ref[...] + bdw_ref[...]
    mu = jnp.mean(y, axis=-1, keepdims=True)
    d = y - mu
    var = jnp.mean(d * d, axis=-1, keepdims=True)
    z = d * lax.rsqrt(var + EPS) * gln_ref[...] + bln_ref[...]
    conv_ref[0] = (z * _sigmoid(z)).astype(conv_ref.dtype)


def _in_proj(x, wq, wk, wv, wu, bglu, cosT, sinT, c, sa, sb, w_dw, b_dw, g_ln, b_ln):
    B, S, _ = x.shape
    nb = S // MOBA_BLOCK
    const = lambda shape: pl.BlockSpec(shape, lambda b, n: (0,) * len(shape))
    row = const((1, CONV_WIDTH))
    return pl.pallas_call(
        _in_proj_kernel,
        grid=(B, nb),
        in_specs=[
            pl.BlockSpec((1, MOBA_BLOCK, D_MODEL), lambda b, n: (b, n, 0)),
            const(wq.shape), const(wk.shape), const(wv.shape), const(wu.shape),
            const(bglu.shape),
            pl.BlockSpec((ROT_HALF, MOBA_BLOCK), lambda b, n: (0, n)),
            pl.BlockSpec((ROT_HALF, MOBA_BLOCK), lambda b, n: (0, n)),
            pl.BlockSpec((MOBA_BLOCK, K_LANES), lambda b, n: (n, 0)),
            pl.BlockSpec((MOBA_BLOCK, K_LANES), lambda b, n: (n, 0)),
            pl.BlockSpec((MOBA_BLOCK, K_LANES), lambda b, n: (n, 0)),
            const(w_dw.shape), row, row, row,
        ],
        out_specs=[
            pl.BlockSpec((1, N_HEADS * Q_ROWS, MOBA_BLOCK), lambda b, n: (b, 0, n)),
            pl.BlockSpec((1, 1, MOBA_BLOCK, N_HEADS * K_LANES), lambda b, n: (b, n, 0, 0)),
            pl.BlockSpec((1, 1, N_HEADS * V_ROWS, MOBA_BLOCK), lambda b, n: (b, n, 0, 0)),
            pl.BlockSpec((1, nb, N_HEADS * K_LANES), lambda b, n: (b, 0, 0)),
            pl.BlockSpec((1, MOBA_BLOCK, CONV_WIDTH), lambda b, n: (b, n, 0)),
        ],
        out_shape=[
            jax.ShapeDtypeStruct((B, N_HEADS * Q_ROWS, S), jnp.bfloat16),
            jax.ShapeDtypeStruct((B, nb, MOBA_BLOCK, N_HEADS * K_LANES), jnp.bfloat16),
            jax.ShapeDtypeStruct((B, nb, N_HEADS * V_ROWS, MOBA_BLOCK), jnp.bfloat16),
            jax.ShapeDtypeStruct((B, nb, N_HEADS * K_LANES), jnp.float32),
            jax.ShapeDtypeStruct((B, S, CONV_WIDTH), jnp.bfloat16),
        ],
        scratch_shapes=[
            pltpu.VMEM((SUBLANES, CONV_HALO + MOBA_BLOCK, CONV_WIDTH), jnp.float32),
            pltpu.VMEM((MOBA_BLOCK, CONV_WIDTH), jnp.float32),
        ],
        compiler_params=pltpu.CompilerParams(
            dimension_semantics=("parallel", "arbitrary"), vmem_limit_bytes=VMEM_LIMIT),
        name="in_proj",
    )(x, wq, wk, wv, wu, bglu, cosT, sinT, c, sa, sb, w_dw, b_dw, g_ln, b_ln)


HEADS_PER_STEP = 4


def _moba_kernel(qT_ref, k_ref, vT_ref, kmean_ref, o_ref, s_ref, p_ref):
    i = pl.program_id(2)
    nb = k_ref.shape[1]
    kpos = lax.broadcasted_iota(jnp.int32, (MOBA_BLOCK, MOBA_BLOCK), 0)
    qpos = lax.broadcasted_iota(jnp.int32, (MOBA_BLOCK, MOBA_BLOCK), 1)
    causal = kpos <= qpos
    blk = lax.broadcasted_iota(jnp.int32, (nb, MOBA_BLOCK), 0)
    pad = jnp.zeros((Q_ROWS - HEAD_DIM - nb, MOBA_BLOCK), jnp.bfloat16)

    def stage_logits(h, n, qa):
        k_blk = k_ref[0, n, :, h * K_LANES:(h + 1) * K_LANES]
        return jnp.dot(k_blk, qa, preferred_element_type=jnp.float32)

    def stage_values(h, n, p):
        vT_blk = vT_ref[0, n, h * V_ROWS:(h + 1) * V_ROWS, :]
        return jnp.dot(vT_blk, p, preferred_element_type=jnp.float32)

    def stage_softmax(s, m, own):
        if own:
            s = jnp.where(causal, s, -jnp.inf)
        m_new = jnp.maximum(m, jnp.max(s, axis=0, keepdims=True))
        return m_new, jnp.exp2(m - m_new), jnp.exp2(s - m_new).astype(jnp.bfloat16)

    qas = []
    for h in range(HEADS_PER_STEP):
        qT = qT_ref[0, h * Q_ROWS:(h + 1) * Q_ROWS, :]

        km = kmean_ref[0, :, h * K_LANES:(h + 1) * K_LANES]
        km_hi = km.astype(jnp.bfloat16)
        km_lo = (km - km_hi.astype(jnp.float32)).astype(jnp.bfloat16)
        g2 = jnp.dot(jnp.concatenate([km_hi, km_lo], axis=0), qT, preferred_element_type=jnp.float32)
        gate = g2[:nb] + g2[nb:]

        rem = jnp.where(blk < i, gate, -jnp.inf)
        sel = blk == i
        for _ in range(MOBA_TOP_K):
            mx = jnp.max(rem, axis=0, keepdims=True)
            first = jnp.min(jnp.where(rem == mx, blk, nb), axis=0, keepdims=True)
            pick = (blk == first) & (mx > -jnp.inf)
            sel = sel | pick
            rem = jnp.where(pick, -jnp.inf, rem)
        bias = jnp.where(sel, 0.0, MASKED).astype(jnp.bfloat16)

        qa = jnp.concatenate([qT[:HEAD_DIM], bias, pad], axis=0)
        qas.append(qa)

    heads = range(HEADS_PER_STEP)
    for h in heads:
        s_ref[h] = stage_logits(h, 0, qas[h])
        p_ref[h] = jnp.zeros((MOBA_BLOCK, MOBA_BLOCK), jnp.bfloat16)

    def values_behind(t, carry):
        pvs = [stage_values(h, jnp.maximum(t - 1, 0), p_ref[h]) for h in heads]
        return [acc * alpha + pv for (_, alpha, acc), pv in zip(carry, pvs)]

    def body(t, carry):
        ahead = [stage_logits(h, t + 1, qas[h]) for h in heads]
        accs = values_behind(t, carry)
        new = []
        for h in heads:
            m_new, alpha, p = stage_softmax(s_ref[h], carry[h][0], own=False)
            p_ref[h] = p
            new.append((m_new, alpha, accs[h]))
        for h in heads:
            s_ref[h] = ahead[h]
        return tuple(new)

    init = (jnp.full((1, MOBA_BLOCK), -jnp.inf, jnp.float32), jnp.ones((1, MOBA_BLOCK), jnp.float32),
            jnp.zeros((V_ROWS, MOBA_BLOCK), jnp.float32))
    carry = lax.fori_loop(0, i // 2, lambda tt, c: body(2 * tt + 1, body(2 * tt, c)),
                          (init,) * HEADS_PER_STEP)
    carry = lax.cond(i % 2 == 1, lambda c: body(i - 1, c), lambda c: c, carry)
    accs = values_behind(i, carry)
    for h in heads:
        _, alpha, p = stage_softmax(s_ref[h], carry[h][0], own=True)
        acc = accs[h] * alpha + stage_values(h, i, p)
        o_ref[0, h * HEAD_DIM:(h + 1) * HEAD_DIM, :] = (
            acc[:HEAD_DIM] * (1.0 / acc[HEAD_DIM:HEAD_DIM + 1])).astype(o_ref.dtype)


def _moba_attention(qT, k, vT, kmean):
    B, _, S = qT.shape
    nb = S // MOBA_BLOCK
    hp = HEADS_PER_STEP
    return pl.pallas_call(
        _moba_kernel,
        grid=(B, N_HEADS // hp, nb),
        in_specs=[
            pl.BlockSpec((1, hp * Q_ROWS, MOBA_BLOCK), lambda b, g, i: (b, g, i)),
            pl.BlockSpec((1, nb, MOBA_BLOCK, hp * K_LANES), lambda b, g, i: (b, 0, 0, g)),
            pl.BlockSpec((1, nb, hp * V_ROWS, MOBA_BLOCK), lambda b, g, i: (b, 0, g, 0)),
            pl.BlockSpec((1, nb, hp * K_LANES), lambda b, g, i: (b, 0, g)),
        ],
        out_specs=pl.BlockSpec((1, hp * HEAD_DIM, MOBA_BLOCK), lambda b, g, i: (b, g, i)),
        out_shape=jax.ShapeDtypeStruct((B, ATTN_WIDTH, S), jnp.bfloat16),
        scratch_shapes=[
            pltpu.VMEM((hp, MOBA_BLOCK, MOBA_BLOCK), jnp.float32),
            pltpu.VMEM((hp, MOBA_BLOCK, MOBA_BLOCK), jnp.bfloat16),
        ],
        compiler_params=pltpu.CompilerParams(
            dimension_semantics=("parallel", "parallel", "arbitrary"), vmem_limit_bytes=VMEM_LIMIT),
        name="moba_attn",
    )(qT, k, vT, kmean)


MLP_TILE = 512
FF_CHUNK = 2048


def _out_mlp_kernel(x_ref, aT_ref, cv_ref, woa_ref, woc_ref, g_ref, w1_ref, w2_ref, gf_ref, o_ref,
                    h_ref, *, final_norm):
    attn = aT_ref[0].T
    h = x_ref[...]
    h = h + jnp.dot(attn, woa_ref[...], preferred_element_type=jnp.float32)
    h = h + jnp.dot(cv_ref[...], woc_ref[...], preferred_element_type=jnp.float32)
    h_ref[...] = h
    hn = _rms(h, g_ref[...]).astype(jnp.bfloat16)

    def chunk(c, carry):
        ff = jnp.maximum(jnp.dot(hn, w1_ref[c], preferred_element_type=jnp.float32), 0.0)
        h_ref[...] += jnp.dot((ff * ff).astype(jnp.bfloat16), w2_ref[c], preferred_element_type=jnp.float32)
        return carry

    lax.fori_loop(0, D_FF // FF_CHUNK, chunk, 0)
    h = h_ref[...]
    o_ref[...] = _rms(h, gf_ref[...]) if final_norm else h


def _out_mlp(x2d, attnT, conv2d, woa, woc, g_mlp, w1, w2, g_final, final_norm):
    T = x2d.shape[0]
    S = attnT.shape[2]
    per = S // MLP_TILE
    const = lambda shape: pl.BlockSpec(shape, lambda t: (0,) * len(shape), pipeline_mode=pl.Buffered(1))
    return pl.pallas_call(
        functools.partial(_out_mlp_kernel, final_norm=final_norm),
        grid=(T // MLP_TILE,),
        in_specs=[
            pl.BlockSpec((MLP_TILE, D_MODEL), lambda t: (t, 0)),
            pl.BlockSpec((1, ATTN_WIDTH, MLP_TILE), lambda t: (t // per, 0, t % per)),
            pl.BlockSpec((MLP_TILE, CONV_WIDTH), lambda t: (t, 0)),
            const(woa.shape), const(woc.shape), const((1, D_MODEL)),
            const(w1.shape), const(w2.shape), const((1, D_MODEL)),
        ],
        out_specs=pl.BlockSpec((MLP_TILE, D_MODEL), lambda t: (t, 0)),
        out_shape=jax.ShapeDtypeStruct((T, D_MODEL), jnp.float32),
        scratch_shapes=[pltpu.VMEM((MLP_TILE, D_MODEL), jnp.float32)],
        compiler_params=pltpu.CompilerParams(
            dimension_semantics=("parallel",), vmem_limit_bytes=VMEM_LIMIT),
        name="out_mlp",
    )(x2d, attnT, conv2d, woa, woc, g_mlp, w1, w2, g_final)


def _rope_tables(S):
    inv_freq = ROPE_THETA ** (-jnp.arange(ROT_HALF, dtype=jnp.float32) * 2.0 / ROT_DIM)
    ang = jnp.arange(S, dtype=jnp.float32)[:, None] * inv_freq[None, :]
    cos, sin = jnp.cos(ang), jnp.sin(ang)
    zeros = jnp.zeros((S, K_LANES - ROT_DIM), jnp.float32)
    zero8 = jnp.zeros((S, ROT_HALF), jnp.float32)
    c = jnp.concatenate([cos, cos, jnp.ones_like(zeros)], axis=1)
    sa = jnp.concatenate([-sin, zero8, zeros], axis=1)
    sb = jnp.concatenate([zero8, sin, zeros], axis=1)
    return cos.T, sin.T, c, sa, sb


def _split_w_in(w, g):
    bf = jnp.bfloat16
    wT = (w * g[:, None]).T
    wq = wT[:ATTN_WIDTH].reshape(N_HEADS, HEAD_DIM, D_MODEL) * (HEAD_DIM ** -0.5 * LOG2E)
    wq = jnp.pad(wq, ((0, 0), (0, Q_ROWS - HEAD_DIM), (0, 0))).reshape(N_HEADS * Q_ROWS, D_MODEL)
    wk = wT[ATTN_WIDTH:2 * ATTN_WIDTH].reshape(N_HEADS, HEAD_DIM, D_MODEL)
    wk = jnp.pad(wk, ((0, 0), (0, K_LANES - HEAD_DIM), (0, 0))).reshape(N_HEADS * K_LANES, D_MODEL)
    wv = wT[2 * ATTN_WIDTH:3 * ATTN_WIDTH].reshape(N_HEADS, HEAD_DIM, D_MODEL)
    wv = jnp.pad(wv, ((0, 0), (0, V_ROWS - HEAD_DIM), (0, 0))).reshape(N_HEADS * V_ROWS, D_MODEL)
    wu = wT[3 * ATTN_WIDTH:].reshape(2, CONV_WIDTH // LANES, LANES, D_MODEL)
    wu = wu.transpose(1, 0, 2, 3).reshape(CONV_WIDTH // LANES, 2 * LANES, D_MODEL)
    return wq.astype(bf), wk.astype(bf), wv.astype(bf), wu.astype(bf)


def _split_b_glu(b):
    return b.reshape(2, CONV_WIDTH // LANES, 1, LANES).transpose(1, 2, 0, 3).reshape(CONV_WIDTH // LANES, 1, 2 * LANES)


def kernel(x, g_mix_norm, w_in, b_glu, w_dw, b_dw, g_conv_ln, b_conv_ln, w_out, g_mlp_norm,
           w_mlp_in, w_mlp_out, g_final):
    B, S, D = x.shape
    depth = w_in.shape[0]
    bf = jnp.bfloat16
    tables = _rope_tables(S)
    h = x
    for l in range(depth):
        wq, wk, wv, wu = _split_w_in(w_in[l], g_mix_norm[l])
        qT, k, vT, kmean, conv = _in_proj(
            h, wq, wk, wv, wu, _split_b_glu(b_glu[l]), *tables,
            w_dw[l][:, 0, :], b_dw[l][None], g_conv_ln[l][None], b_conv_ln[l][None])
        attnT = _moba_attention(qT, k, vT, kmean)
        last = l == depth - 1
        h = _out_mlp(
            h.reshape(B * S, D), attnT, conv.reshape(B * S, CONV_WIDTH),
            w_out[l][:ATTN_WIDTH].astype(bf), w_out[l][ATTN_WIDTH:].astype(bf), g_mlp_norm[l][None],
            w_mlp_in[l].reshape(D, D_FF // FF_CHUNK, FF_CHUNK).transpose(1, 0, 2).astype(bf),
            w_mlp_out[l].reshape(D_FF // FF_CHUNK, FF_CHUNK, D).astype(bf), g_final[None], final_norm=last,
        ).reshape(B, S, D)
    return h
```

```python
import functools

import jax
import jax.numpy as jnp
from jax import lax
from jax.experimental import pallas as pl
from jax.experimental.pallas import tpu as pltpu

D_MODEL = 1024
ATTN_WIDTH = 512
CONV_WIDTH = 512
HEAD_DIM = 64
N_HEADS = 8
ROT_DIM = 16
ROT_HALF = 8
ROPE_THETA = 500000.0
MOBA_BLOCK = 256
MOBA_TOP_K = 3
CONV_K = 31
D_FF = 4096
EPS = 1e-6

Q_ROWS = 128
K_LANES = 128
BIAS_ROW0 = HEAD_DIM
V_ROWS = 80
MASKED = -1e30
LOG2E = 1.4426950408889634

VMEM_LIMIT = 56 * 1024 * 1024

_NT = (((1,), (1,)), ((), ()))


def _rms(x, g):
    return x * lax.rsqrt(jnp.mean(x * x, axis=-1, keepdims=True) + EPS) * g


def _sigmoid(x):
    return 1.0 / (1.0 + jnp.exp(-x))


CONV_HALO = 32
CONV_ROWS = 64
LANES = 128
SUBLANES = 8


def _in_proj_kernel(x_ref, wq_ref, wk_ref, wv_ref, wu_ref, bglu_ref,
                    cosT_ref, sinT_ref, c_ref, sa_ref, sb_ref, wdw_ref, bdw_ref, gln_ref, bln_ref,
                    qT_ref, k_ref, vT_ref, kmean_ref, conv_ref, win_ref, y_ref):
    n = pl.program_id(1)

    @pl.when(n == 0)
    def _():
        win_ref[0, 0:CONV_HALO, :] = jnp.zeros((CONV_HALO, CONV_WIDTH), jnp.float32)

    x = x_ref[0]
    xb = (x * lax.rsqrt(jnp.mean(x * x, axis=-1, keepdims=True) + EPS)).astype(jnp.bfloat16)

    rows = CONV_HALO + MOBA_BLOCK
    off = CONV_HALO - (CONV_K - 1)
    for ch in range(CONV_WIDTH // LANES):
        lanes = slice(ch * LANES, (ch + 1) * LANES)
        u = jnp.dot(xb, wu_ref[:, 2 * ch * LANES:2 * (ch + 1) * LANES], preferred_element_type=jnp.float32) + bglu_ref[ch]
        win_ref[0, CONV_HALO:, lanes] = u[:, :LANES] * _sigmoid(u[:, LANES:])
        for b in range(1, SUBLANES):
            win_ref[b, 0:rows - SUBLANES, lanes] = win_ref[0, pl.ds(b, rows - SUBLANES), lanes]
        for r in range(MOBA_BLOCK // CONV_ROWS):
            acc = jnp.zeros((CONV_ROWS, LANES), jnp.float32)
            for j in range(CONV_K):
                a, b = divmod(off + j, SUBLANES)
                start = r * CONV_ROWS + a * SUBLANES
                acc = acc + win_ref[b, start:start + CONV_ROWS, lanes] * wdw_ref[j:j + 1, lanes]
            y_ref[r * CONV_ROWS:(r + 1) * CONV_ROWS, lanes] = acc
        win_ref[0, 0:CONV_HALO, lanes] = win_ref[0, MOBA_BLOCK:rows, lanes]

    qT = lax.dot_general(wq_ref[...], xb, _NT, preferred_element_type=jnp.float32)
    q3 = qT.reshape(N_HEADS, Q_ROWS, MOBA_BLOCK)
    cos = cosT_ref[...][None]
    sin = sinT_ref[...][None]
    x1 = q3[:, 0:ROT_HALF]
    x2 = q3[:, ROT_HALF:ROT_DIM]
    q3 = jnp.concatenate([x1 * cos - x2 * sin, x2 * cos + x1 * sin, q3[:, ROT_DIM:]], axis=1)
    qT_ref[0] = q3.reshape(N_HEADS * Q_ROWS, MOBA_BLOCK).astype(jnp.bfloat16)

    k = jnp.dot(xb, wk_ref[...], preferred_element_type=jnp.float32)
    c = c_ref[...]
    sa = sa_ref[...]
    sb = sb_ref[...]
    parts = []
    for h in range(N_HEADS):
        kh = k[:, h * K_LANES:(h + 1) * K_LANES]
        parts.append(kh * c + pltpu.roll(kh, K_LANES - ROT_HALF, 1) * sa + pltpu.roll(kh, ROT_HALF, 1) * sb)
    k = jnp.concatenate(parts, axis=1)
    kmean_ref[0, pl.ds(n, 1), :] = jnp.mean(k, axis=0, keepdims=True)
    lane = lax.broadcasted_iota(jnp.int32, (1, N_HEADS * K_LANES), 1) % K_LANES
    k_ref[0, 0] = (k + jnp.where(lane == BIAS_ROW0 + n, 1.0, 0.0)).astype(jnp.bfloat16)

    vT = lax.dot_general(wv_ref[...], xb, _NT, preferred_element_type=jnp.float32)
    vT_ref[0, 0] = vT.astype(jnp.bfloat16)
    first = lax.broadcasted_iota(jnp.int32, (V_ROWS - HEAD_DIM, MOBA_BLOCK), 0) == 0
    ones_rows = jnp.where(first, 1.0, 0.0).astype(jnp.bfloat16)
    for h in range(N_HEADS):
        vT_ref[0, 0, h * V_ROWS + HEAD_DIM:(h + 1) * V_ROWS, :] = ones_rows

    y = y_ref[...] + bdw_ref[...]
    mu = jnp.mean(y, axis=-1, keepdims=True)
    d = y - mu
    var = jnp.mean(d * d, axis=-1, keepdims=True)
    z = d * lax.rsqrt(var + EPS) * gln_ref[...] + bln_ref[...]
    conv_ref[0] = (z * _sigmoid(z)).astype(conv_ref.dtype)


def _in_proj(x, wq, wk, wv, wu, bglu, cosT, sinT, c, sa, sb, w_dw, b_dw, g_ln, b_ln):
    B, S, _ = x.shape
    nb = S // MOBA_BLOCK
    const = lambda shape: pl.BlockSpec(shape, lambda b, n: (0,) * len(shape))
    row = const((1, CONV_WIDTH))
    return pl.pallas_call(
        _in_proj_kernel,
        grid=(B, nb),
        in_specs=[
            pl.BlockSpec((1, MOBA_BLOCK, D_MODEL), lambda b, n: (b, n, 0)),
            const(wq.shape), const(wk.shape), const(wv.shape), const(wu.shape),
            const(bglu.shape),
            pl.BlockSpec((ROT_HALF, MOBA_BLOCK), lambda b, n: (0, n)),
            pl.BlockSpec((ROT_HALF, MOBA_BLOCK), lambda b, n: (0, n)),
            pl.BlockSpec((MOBA_BLOCK, K_LANES), lambda b, n: (n, 0)),
            pl.BlockSpec((MOBA_BLOCK, K_LANES), lambda b, n: (n, 0)),
            pl.BlockSpec((MOBA_BLOCK, K_LANES), lambda b, n: (n, 0)),
            const(w_dw.shape), row, row, row,
        ],
        out_specs=[
            pl.BlockSpec((1, N_HEADS * Q_ROWS, MOBA_BLOCK), lambda b, n: (b, 0, n)),
            pl.BlockSpec((1, 1, MOBA_BLOCK, N_HEADS * K_LANES), lambda b, n: (b, n, 0, 0)),
            pl.BlockSpec((1, 1, N_HEADS * V_ROWS, MOBA_BLOCK), lambda b, n: (b, n, 0, 0)),
            pl.BlockSpec((1, nb, N_HEADS * K_LANES), lambda b, n: (b, 0, 0)),
            pl.BlockSpec((1, MOBA_BLOCK, CONV_WIDTH), lambda b, n: (b, n, 0)),
        ],
        out_shape=[
            jax.ShapeDtypeStruct((B, N_HEADS * Q_ROWS, S), jnp.bfloat16),
            jax.ShapeDtypeStruct((B, nb, MOBA_BLOCK, N_HEADS * K_LANES), jnp.bfloat16),
            jax.ShapeDtypeStruct((B, nb, N_HEADS * V_ROWS, MOBA_BLOCK), jnp.bfloat16),
            jax.ShapeDtypeStruct((B, nb, N_HEADS * K_LANES), jnp.float32),
            jax.ShapeDtypeStruct((B, S, CONV_WIDTH), jnp.bfloat16),
        ],
        scratch_shapes=[
            pltpu.VMEM((SUBLANES, CONV_HALO + MOBA_BLOCK, CONV_WIDTH), jnp.float32),
            pltpu.VMEM((MOBA_BLOCK, CONV_WIDTH), jnp.float32),
        ],
        compiler_params=pltpu.CompilerParams(
            dimension_semantics=("parallel", "arbitrary"), vmem_limit_bytes=VMEM_LIMIT),
        name="in_proj",
    )(x, wq, wk, wv, wu, bglu, cosT, sinT, c, sa, sb, w_dw, b_dw, g_ln, b_ln)


HEADS_PER_STEP = 4


def _moba_kernel(qT_ref, k_ref, vT_ref, kmean_ref, o_ref, s_ref, p_ref):
    i = pl.program_id(2)
    nb = k_ref.shape[1]
    kpos = lax.broadcasted_iota(jnp.int32, (MOBA_BLOCK, MOBA_BLOCK), 0)
    qpos = lax.broadcasted_iota(jnp.int32, (MOBA_BLOCK, MOBA_BLOCK), 1)
    causal = kpos <= qpos
    blk = lax.broadcasted_iota(jnp.int32, (nb, MOBA_BLOCK), 0)
    pad = jnp.zeros((Q_ROWS - HEAD_DIM - nb, MOBA_BLOCK), jnp.bfloat16)

    def stage_logits(h, n, qa):
        k_blk = k_ref[0, n, :, h * K_LANES:(h + 1) * K_LANES]
        return jnp.dot(k_blk, qa, preferred_element_type=jnp.float32)

    def stage_values(h, n, p):
        vT_blk = vT_ref[0, n, h * V_ROWS:(h + 1) * V_ROWS, :]
        return jnp.dot(vT_blk, p, preferred_element_type=jnp.float32)

    def stage_softmax(s, m, own):
        if own:
            s = jnp.where(causal, s, -jnp.inf)
        m_new = jnp.maximum(m, jnp.max(s, axis=0, keepdims=True))
        return m_new, jnp.exp2(m - m_new), jnp.exp2(s - m_new).astype(jnp.bfloat16)

    qas = []
    for h in range(HEADS_PER_STEP):
        qT = qT_ref[0, h * Q_ROWS:(h + 1) * Q_ROWS, :]

        km = kmean_ref[0, :, h * K_LANES:(h + 1) * K_LANES]
        km_hi = km.astype(jnp.bfloat16)
        km_lo = (km - km_hi.astype(jnp.float32)).astype(jnp.bfloat16)
        g2 = jnp.dot(jnp.concatenate([km_hi, km_lo], axis=0), qT, preferred_element_type=jnp.float32)
        gate = g2[:nb] + g2[nb:]

        rem = jnp.where(blk < i, gate, -jnp.inf)
        sel = blk == i
        for _ in range(MOBA_TOP_K):
            mx = jnp.max(rem, axis=0, keepdims=True)
            first = jnp.min(jnp.where(rem == mx, blk, nb), axis=0, keepdims=True)
            pick = (blk == first) & (mx > -jnp.inf)
            sel = sel | pick
            rem = jnp.where(pick, -jnp.inf, rem)
        bias = jnp.where(sel, 0.0, MASKED).astype(jnp.bfloat16)

        qa = jnp.concatenate([qT[:HEAD_DIM], bias, pad], axis=0)
        qas.append(qa)

    heads = range(HEADS_PER_STEP)
    for h in heads:
        s_ref[h] = stage_logits(h, 0, qas[h])
        p_ref[h] = jnp.zeros((MOBA_BLOCK, MOBA_BLOCK), jnp.bfloat16)

    def values_behind(t, carry):
        pvs = [stage_values(h, jnp.maximum(t - 1, 0), p_ref[h]) for h in heads]
        return [acc * alpha + pv for (_, alpha, acc), pv in zip(carry, pvs)]

    def body(t, carry):
        ahead = [stage_logits(h, t + 1, qas[h]) for h in heads]
        accs = values_behind(t, carry)
        new = []
        for h in heads:
            m_new, alpha, p = stage_softmax(s_ref[h], carry[h][0], own=False)
            p_ref[h] = p
            new.append((m_new, alpha, accs[h]))
        for h in heads:
            s_ref[h] = ahead[h]
        return tuple(new)

    init = (jnp.full((1, MOBA_BLOCK), -jnp.inf, jnp.float32), jnp.ones((1, MOBA_BLOCK), jnp.float32),
            jnp.zeros((V_ROWS, MOBA_BLOCK), jnp.float32))
    carry = lax.fori_loop(0, i // 2, lambda tt, c: body(2 * tt + 1, body(2 * tt, c)),
                          (init,) * HEADS_PER_STEP)
    carry = lax.cond(i % 2 == 1, lambda c: body(i - 1, c), lambda c: c, carry)
    accs = values_behind(i, carry)
    for h in heads:
        _, alpha, p = stage_softmax(s_ref[h], carry[h][0], own=True)
        acc = accs[h] * alpha + stage_values(h, i, p)
        o_ref[0, h * HEAD_DIM:(h + 1) * HEAD_DIM, :] = (
            acc[:HEAD_DIM] * (1.0 / acc[HEAD_DIM:HEAD_DIM + 1])).astype(o_ref.dtype)


def _moba_attention(qT, k, vT, kmean):
    B, _, S = qT.shape
    nb = S // MOBA_BLOCK
    hp = HEADS_PER_STEP
    return pl.pallas_call(
        _moba_kernel,
        grid=(B, N_HEADS // hp, nb),
        in_specs=[
            pl.BlockSpec((1, hp * Q_ROWS, MOBA_BLOCK), lambda b, g, i: (b, g, i)),
            pl.BlockSpec((1, nb, MOBA_BLOCK, hp * K_LANES), lambda b, g, i: (b, 0, 0, g)),
            pl.BlockSpec((1, nb, hp * V_ROWS, MOBA_BLOCK), lambda b, g, i: (b, 0, g, 0)),
            pl.BlockSpec((1, nb, hp * K_LANES), lambda b, g, i: (b, 0, g)),
        ],
        out_specs=pl.BlockSpec((1, hp * HEAD_DIM, MOBA_BLOCK), lambda b, g, i: (b, g, i)),
        out_shape=jax.ShapeDtypeStruct((B, ATTN_WIDTH, S), jnp.bfloat16),
        scratch_shapes=[
            pltpu.VMEM((hp, MOBA_BLOCK, MOBA_BLOCK), jnp.float32),
            pltpu.VMEM((hp, MOBA_BLOCK, MOBA_BLOCK), jnp.bfloat16),
        ],
        compiler_params=pltpu.CompilerParams(
            dimension_semantics=("parallel", "parallel", "arbitrary"), vmem_limit_bytes=VMEM_LIMIT),
        name="moba_attn",
    )(qT, k, vT, kmean)


MLP_TILE = 512
FF_CHUNK = 2048


def _out_mlp_kernel(x_ref, aT_ref, cv_ref, woa_ref, woc_ref, g_ref, w1_ref, w2_ref, gf_ref, o_ref,
                    h_ref, *, final_norm):
    attn = aT_ref[0].T
    h = x_ref[...]
    h = h + jnp.dot(attn, woa_ref[...], preferred_element_type=jnp.float32)
    h = h + jnp.dot(cv_ref[...], woc_ref[...], preferred_element_type=jnp.float32)
    h_ref[...] = h
    hn = _rms(h, g_ref[...]).astype(jnp.bfloat16)

    def chunk(c, carry):
        cols = pl.ds(pl.multiple_of(c * FF_CHUNK, FF_CHUNK), FF_CHUNK)
        ff = jnp.maximum(jnp.dot(hn, w1_ref[:, cols], preferred_element_type=jnp.float32), 0.0)
        h_ref[...] += jnp.dot((ff * ff).astype(jnp.bfloat16), w2_ref[c], preferred_element_type=jnp.float32)
        return carry

    lax.fori_loop(0, D_FF // FF_CHUNK, chunk, 0)
    h = h_ref[...]
    o_ref[...] = _rms(h, gf_ref[...]) if final_norm else h


def _out_mlp(x2d, attnT, conv2d, woa, woc, g_mlp, w1, w2, g_final, final_norm):
    T = x2d.shape[0]
    S = attnT.shape[2]
    per = S // MLP_TILE
    const = lambda shape: pl.BlockSpec(shape, lambda t: (0,) * len(shape), pipeline_mode=pl.Buffered(1))
    return pl.pallas_call(
        functools.partial(_out_mlp_kernel, final_norm=final_norm),
        grid=(T // MLP_TILE,),
        in_specs=[
            pl.BlockSpec((MLP_TILE, D_MODEL), lambda t: (t, 0)),
            pl.BlockSpec((1, ATTN_WIDTH, MLP_TILE), lambda t: (t // per, 0, t % per)),
            pl.BlockSpec((MLP_TILE, CONV_WIDTH), lambda t: (t, 0)),
            const(woa.shape), const(woc.shape), const((1, D_MODEL)),
            const(w1.shape), const(w2.shape), const((1, D_MODEL)),
        ],
        out_specs=pl.BlockSpec((MLP_TILE, D_MODEL), lambda t: (t, 0)),
        out_shape=jax.ShapeDtypeStruct((T, D_MODEL), jnp.float32),
        scratch_shapes=[pltpu.VMEM((MLP_TILE, D_MODEL), jnp.float32)],
        compiler_params=pltpu.CompilerParams(
            dimension_semantics=("parallel",), vmem_limit_bytes=VMEM_LIMIT),
        name="out_mlp",
    )(x2d, attnT, conv2d, woa, woc, g_mlp, w1, w2, g_final)


def _rope_tables(S):
    inv_freq = ROPE_THETA ** (-jnp.arange(ROT_HALF, dtype=jnp.float32) * 2.0 / ROT_DIM)
    ang = jnp.arange(S, dtype=jnp.float32)[:, None] * inv_freq[None, :]
    cos, sin = jnp.cos(ang), jnp.sin(ang)
    zeros = jnp.zeros((S, K_LANES - ROT_DIM), jnp.float32)
    zero8 = jnp.zeros((S, ROT_HALF), jnp.float32)
    c = jnp.concatenate([cos, cos, jnp.ones_like(zeros)], axis=1)
    sa = jnp.concatenate([-sin, zero8, zeros], axis=1)
    sb = jnp.concatenate([zero8, sin, zeros], axis=1)
    return cos.T, sin.T, c, sa, sb


def _split_w_in(w, g):
    bf = jnp.bfloat16
    w = w * g[:, None]
    wq = w[:, :ATTN_WIDTH].T.reshape(N_HEADS, HEAD_DIM, D_MODEL) * (HEAD_DIM ** -0.5 * LOG2E)
    wq = jnp.pad(wq, ((0, 0), (0, Q_ROWS - HEAD_DIM), (0, 0))).reshape(N_HEADS * Q_ROWS, D_MODEL)
    wk = w[:, ATTN_WIDTH:2 * ATTN_WIDTH].reshape(D_MODEL, N_HEADS, HEAD_DIM)
    wk = jnp.pad(wk, ((0, 0), (0, 0), (0, K_LANES - HEAD_DIM))).reshape(D_MODEL, N_HEADS * K_LANES)
    wv = w[:, 2 * ATTN_WIDTH:3 * ATTN_WIDTH].T.reshape(N_HEADS, HEAD_DIM, D_MODEL)
    wv = jnp.pad(wv, ((0, 0), (0, V_ROWS - HEAD_DIM), (0, 0))).reshape(N_HEADS * V_ROWS, D_MODEL)
    wu = w[:, 3 * ATTN_WIDTH:].reshape(D_MODEL, 2, CONV_WIDTH // LANES, LANES)
    wu = wu.transpose(0, 2, 1, 3).reshape(D_MODEL, 2 * CONV_WIDTH)
    return wq.astype(bf), wk.astype(bf), wv.astype(bf), wu.astype(bf)


def _split_b_glu(b):
    return b.reshape(2, CONV_WIDTH // LANES, 1, LANES).transpose(1, 2, 0, 3).reshape(CONV_WIDTH // LANES, 1, 2 * LANES)


def kernel(x, g_mix_norm, w_in, b_glu, w_dw, b_dw, g_conv_ln, b_conv_ln, w_out, g_mlp_norm,
           w_mlp_in, w_mlp_out, g_final):
    B, S, D = x.shape
    depth = w_in.shape[0]
    bf = jnp.bfloat16
    tables = _rope_tables(S)
    h = x
    for l in range(depth):
        wq, wk, wv, wu = _split_w_in(w_in[l], g_mix_norm[l])
        qT, k, vT, kmean, conv = _in_proj(
            h, wq, wk, wv, wu, _split_b_glu(b_glu[l]), *tables,
            w_dw[l][:, 0, :], b_dw[l][None], g_conv_ln[l][None], b_conv_ln[l][None])
        attnT = _moba_attention(qT, k, vT, kmean)
        last = l == depth - 1
        h = _out_mlp(
            h.reshape(B * S, D), attnT, conv.reshape(B * S, CONV_WIDTH),
            w_out[l][:ATTN_WIDTH].astype(bf), w_out[l][ATTN_WIDTH:].astype(bf), g_mlp_norm[l][None],
            w_mlp_in[l].astype(bf),
            w_mlp_out[l].reshape(D_FF // FF_CHUNK, FF_CHUNK, D).astype(bf), g_final[None], final_norm=last,
        ).reshape(B, S, D)
    return h
```

```python
import functools

import jax
import jax.numpy as jnp
from jax import lax
from jax.experimental import pallas as pl
from jax.experimental.pallas import tpu as pltpu

D_MODEL = 1024
ATTN_WIDTH = 512
CONV_WIDTH = 512
HEAD_DIM = 64
N_HEADS = 8
ROT_DIM = 16
ROT_HALF = 8
ROPE_THETA = 500000.0
MOBA_BLOCK = 256
MOBA_TOP_K = 3
CONV_K = 31
D_FF = 4096
EPS = 1e-6

Q_ROWS = 128
K_LANES = 128
BIAS_ROW0 = HEAD_DIM
V_ROWS = 80
MASKED = -1e30
LOG2E = 1.4426950408889634

VMEM_LIMIT = 56 * 1024 * 1024

_NT = (((1,), (1,)), ((), ()))


def _rms(x, g):
    return x * lax.rsqrt(jnp.mean(x * x, axis=-1, keepdims=True) + EPS) * g


def _sigmoid(x):
    return 1.0 / (1.0 + jnp.exp(-x))


CONV_HALO = 32
CONV_ROWS = 64
LANES = 128
SUBLANES = 8


def _in_proj_kernel(x_ref, wq_ref, wk_ref, wv_ref, wu_ref, bglu_ref,
                    cosT_ref, sinT_ref, c_ref, sa_ref, sb_ref, wdw_ref, bdw_ref, gln_ref, bln_ref,
                    qT_ref, k_ref, vT_ref, kmean_ref, conv_ref, win_ref, y_ref):
    n = pl.program_id(1)

    @pl.when(n == 0)
    def _():
        win_ref[0, 0:CONV_HALO, :] = jnp.zeros((CONV_HALO, CONV_WIDTH), jnp.float32)

    x = x_ref[0]
    xb = (x * lax.rsqrt(jnp.mean(x * x, axis=-1, keepdims=True) + EPS)).astype(jnp.bfloat16)

    rows = CONV_HALO + MOBA_BLOCK
    off = CONV_HALO - (CONV_K - 1)
    for ch in range(CONV_WIDTH // LANES):
        lanes = slice(ch * LANES, (ch + 1) * LANES)
        u = jnp.dot(xb, wu_ref[:, 2 * ch * LANES:2 * (ch + 1) * LANES], preferred_element_type=jnp.float32) + bglu_ref[ch]
        win_ref[0, CONV_HALO:, lanes] = u[:, :LANES] * _sigmoid(u[:, LANES:])
        for b in range(1, SUBLANES):
            win_ref[b, 0:rows - SUBLANES, lanes] = win_ref[0, pl.ds(b, rows - SUBLANES), lanes]
        for r in range(MOBA_BLOCK // CONV_ROWS):
            acc = jnp.zeros((CONV_ROWS, LANES), jnp.float32)
            for j in range(CONV_K):
                a, b = divmod(off + j, SUBLANES)
                start = r * CONV_ROWS + a * SUBLANES
                acc = acc + win_ref[b, start:start + CONV_ROWS, lanes] * wdw_ref[j:j + 1, lanes]
            y_ref[r * CONV_ROWS:(r + 1) * CONV_ROWS, lanes] = acc
        win_ref[0, 0:CONV_HALO, lanes] = win_ref[0, MOBA_BLOCK:rows, lanes]

    qT = lax.dot_general(wq_ref[...], xb, _NT, preferred_element_type=jnp.float32)
    q3 = qT.reshape(N_HEADS, Q_ROWS, MOBA_BLOCK)
    cos = cosT_ref[...][None]
    sin = sinT_ref[...][None]
    x1 = q3[:, 0:ROT_HALF]
    x2 = q3[:, ROT_HALF:ROT_DIM]
    q3 = jnp.concatenate([x1 * cos - x2 * sin, x2 * cos + x1 * sin, q3[:, ROT_DIM:]], axis=1)
    qT_ref[0] = q3.reshape(N_HEADS * Q_ROWS, MOBA_BLOCK).astype(jnp.bfloat16)

    k = jnp.dot(xb, wk_ref[...], preferred_element_type=jnp.float32)
    c = c_ref[...]
    sa = sa_ref[...]
    sb = sb_ref[...]
    parts = []
    for h in range(N_HEADS):
        kh = k[:, h * K_LANES:(h + 1) * K_LANES]
        parts.append(kh * c + pltpu.roll(kh, K_LANES - ROT_HALF, 1) * sa + pltpu.roll(kh, ROT_HALF, 1) * sb)
    k = jnp.concatenate(parts, axis=1)
    kmean_ref[0, pl.ds(n, 1), :] = jnp.mean(k, axis=0, keepdims=True)
    lane = lax.broadcasted_iota(jnp.int32, (1, N_HEADS * K_LANES), 1) % K_LANES
    k_ref[0, 0] = (k + jnp.where(lane == BIAS_ROW0 + n, 1.0, 0.0)).astype(jnp.bfloat16)

    vT = lax.dot_general(wv_ref[...], xb, _NT, preferred_element_type=jnp.float32)
    vT_ref[0, 0] = vT.astype(jnp.bfloat16)
    first = lax.broadcasted_iota(jnp.int32, (V_ROWS - HEAD_DIM, MOBA_BLOCK), 0) == 0
    ones_rows = jnp.where(first, 1.0, 0.0).astype(jnp.bfloat16)
    for h in range(N_HEADS):
        vT_ref[0, 0, h * V_ROWS + HEAD_DIM:(h + 1) * V_ROWS, :] = ones_rows

    y = y_ref[...] + bdw_ref[...]
    mu = jnp.mean(y, axis=-1, keepdims=True)
    d = y - mu
    var = jnp.mean(d * d, axis=-1, keepdims=True)
    z = d * lax.rsqrt(var + EPS) * gln_ref[...] + bln_ref[...]
    conv_ref[0] = (z * _sigmoid(z)).astype(conv_ref.dtype)


def _in_proj(x, wq, wk, wv, wu, bglu, cosT, sinT, c, sa, sb, w_dw, b_dw, g_ln, b_ln):
    B, S, _ = x.shape
    nb = S // MOBA_BLOCK
    const = lambda shape: pl.BlockSpec(shape, lambda b, n: (0,) * len(shape))
    row = const((1, CONV_WIDTH))
    return pl.pallas_call(
        _in_proj_kernel,
        grid=(B, nb),
        in_specs=[
            pl.BlockSpec((1, MOBA_BLOCK, D_MODEL), lambda b, n: (b, n, 0)),
            const(wq.shape), const(wk.shape), const(wv.shape), const(wu.shape),
            const(bglu.shape),
            pl.BlockSpec((ROT_HALF, MOBA_BLOCK), lambda b, n: (0, n)),
            pl.BlockSpec((ROT_HALF, MOBA_BLOCK), lambda b, n: (0, n)),
            pl.BlockSpec((MOBA_BLOCK, K_LANES), lambda b, n: (n, 0)),
            pl.BlockSpec((MOBA_BLOCK, K_LANES), lambda b, n: (n, 0)),
            pl.BlockSpec((MOBA_BLOCK, K_LANES), lambda b, n: (n, 0)),
            const(w_dw.shape), row, row, row,
        ],
        out_specs=[
            pl.BlockSpec((1, N_HEADS * Q_ROWS, MOBA_BLOCK), lambda b, n: (b, 0, n)),
            pl.BlockSpec((1, 1, MOBA_BLOCK, N_HEADS * K_LANES), lambda b, n: (b, n, 0, 0)),
            pl.BlockSpec((1, 1, N_HEADS * V_ROWS, MOBA_BLOCK), lambda b, n: (b, n, 0, 0)),
            pl.BlockSpec((1, nb, N_HEADS * K_LANES), lambda b, n: (b, 0, 0)),
            pl.BlockSpec((1, MOBA_BLOCK, CONV_WIDTH), lambda b, n: (b, n, 0)),
        ],
        out_shape=[
            jax.ShapeDtypeStruct((B, N_HEADS * Q_ROWS, S), jnp.bfloat16),
            jax.ShapeDtypeStruct((B, nb, MOBA_BLOCK, N_HEADS * K_LANES), jnp.bfloat16),
            jax.ShapeDtypeStruct((B, nb, N_HEADS * V_ROWS, MOBA_BLOCK), jnp.bfloat16),
            jax.ShapeDtypeStruct((B, nb, N_HEADS * K_LANES), jnp.float32),
            jax.ShapeDtypeStruct((B, S, CONV_WIDTH), jnp.bfloat16),
        ],
        scratch_shapes=[
            pltpu.VMEM((SUBLANES, CONV_HALO + MOBA_BLOCK, CONV_WIDTH), jnp.float32),
            pltpu.VMEM((MOBA_BLOCK, CONV_WIDTH), jnp.float32),
        ],
        compiler_params=pltpu.CompilerParams(
            dimension_semantics=("parallel", "arbitrary"), vmem_limit_bytes=VMEM_LIMIT),
        name="in_proj",
    )(x, wq, wk, wv, wu, bglu, cosT, sinT, c, sa, sb, w_dw, b_dw, g_ln, b_ln)


HEADS_PER_STEP = 4


def _moba_kernel(qT_ref, k_ref, vT_ref, kmean_ref, o_ref, s_ref, p_ref):
    nb = k_ref.shape[1]
    kpos = lax.broadcasted_iota(jnp.int32, (MOBA_BLOCK, MOBA_BLOCK), 0)
    qpos = lax.broadcasted_iota(jnp.int32, (MOBA_BLOCK, MOBA_BLOCK), 1)
    causal = kpos <= qpos
    blk = lax.broadcasted_iota(jnp.int32, (nb, MOBA_BLOCK), 0)
    pad = jnp.zeros((Q_ROWS - HEAD_DIM - nb, MOBA_BLOCK), jnp.bfloat16)
    heads = range(HEADS_PER_STEP)

    def stage_values(h, n, p):
        vT_blk = vT_ref[0, n, h * V_ROWS:(h + 1) * V_ROWS, :]
        return jnp.dot(vT_blk, p, preferred_element_type=jnp.float32)

    def stage_softmax(s, m, own):
        if own:
            s = jnp.where(causal, s, -jnp.inf)
        m_new = jnp.maximum(m, jnp.max(s, axis=0, keepdims=True))
        return m_new, jnp.exp2(m - m_new), jnp.exp2(s - m_new).astype(jnp.bfloat16)

    def query_block(i, _):
        cols = pl.ds(pl.multiple_of(i * MOBA_BLOCK, MOBA_BLOCK), MOBA_BLOCK)
        qas = []
        for h in heads:
            qT = qT_ref[0, h * Q_ROWS:(h + 1) * Q_ROWS, cols]

            km = kmean_ref[0, :, h * K_LANES:(h + 1) * K_LANES]
            km_hi = km.astype(jnp.bfloat16)
            km_lo = (km - km_hi.astype(jnp.float32)).astype(jnp.bfloat16)
            g2 = jnp.dot(jnp.concatenate([km_hi, km_lo], axis=0), qT, preferred_element_type=jnp.float32)
            gate = g2[:nb] + g2[nb:]

            rem = jnp.where(blk < i, gate, -jnp.inf)
            sel = blk == i
            for _ in range(MOBA_TOP_K):
                mx = jnp.max(rem, axis=0, keepdims=True)
                first = jnp.min(jnp.where(rem == mx, blk, nb), axis=0, keepdims=True)
                pick = (blk == first) & (mx > -jnp.inf)
                sel = sel | pick
                rem = jnp.where(pick, -jnp.inf, rem)
            bias = jnp.where(sel, 0.0, MASKED).astype(jnp.bfloat16)

            qas.append(jnp.concatenate([qT[:HEAD_DIM], bias, pad], axis=0))

        def stage_logits(h, n):
            k_blk = k_ref[0, n, :, h * K_LANES:(h + 1) * K_LANES]
            return jnp.dot(k_blk, qas[h], preferred_element_type=jnp.float32)

        for h in heads:
            s_ref[h] = stage_logits(h, 0)
            p_ref[h] = jnp.zeros((MOBA_BLOCK, MOBA_BLOCK), jnp.bfloat16)

        def values_behind(t, carry):
            pvs = [stage_values(h, jnp.maximum(t - 1, 0), p_ref[h]) for h in heads]
            return [acc * alpha + pv for (_, alpha, acc), pv in zip(carry, pvs)]

        def body(t, carry):
            ahead = [stage_logits(h, t + 1) for h in heads]
            accs = values_behind(t, carry)
            new = []
            for h in heads:
                m_new, alpha, p = stage_softmax(s_ref[h], carry[h][0], own=False)
                p_ref[h] = p
                new.append((m_new, alpha, accs[h]))
            for h in heads:
                s_ref[h] = ahead[h]
            return tuple(new)

        init = (jnp.full((1, MOBA_BLOCK), -jnp.inf, jnp.float32), jnp.ones((1, MOBA_BLOCK), jnp.float32),
                jnp.zeros((V_ROWS, MOBA_BLOCK), jnp.float32))
        carry = lax.fori_loop(0, i // 2, lambda tt, c: body(2 * tt + 1, body(2 * tt, c)),
                              (init,) * HEADS_PER_STEP)
        carry = lax.cond(i % 2 == 1, lambda c: body(i - 1, c), lambda c: c, carry)
        accs = values_behind(i, carry)
        for h in heads:
            _, alpha, p = stage_softmax(s_ref[h], carry[h][0], own=True)
            acc = accs[h] * alpha + stage_values(h, i, p)
            o_ref[0, h * HEAD_DIM:(h + 1) * HEAD_DIM, cols] = (
                acc[:HEAD_DIM] * (1.0 / acc[HEAD_DIM:HEAD_DIM + 1])).astype(o_ref.dtype)
        return 0

    lax.fori_loop(0, nb, query_block, 0)


def _moba_attention(qT, k, vT, kmean):
    B, _, S = qT.shape
    nb = S // MOBA_BLOCK
    hp = HEADS_PER_STEP
    return pl.pallas_call(
        _moba_kernel,
        grid=(B, N_HEADS // hp),
        in_specs=[
            pl.BlockSpec((1, hp * Q_ROWS, S), lambda b, g: (b, g, 0)),
            pl.BlockSpec((1, nb, MOBA_BLOCK, hp * K_LANES), lambda b, g: (b, 0, 0, g)),
            pl.BlockSpec((1, nb, hp * V_ROWS, MOBA_BLOCK), lambda b, g: (b, 0, g, 0)),
            pl.BlockSpec((1, nb, hp * K_LANES), lambda b, g: (b, 0, g)),
        ],
        out_specs=pl.BlockSpec((1, hp * HEAD_DIM, S), lambda b, g: (b, g, 0)),
        out_shape=jax.ShapeDtypeStruct((B, ATTN_WIDTH, S), jnp.bfloat16),
        scratch_shapes=[
            pltpu.VMEM((hp, MOBA_BLOCK, MOBA_BLOCK), jnp.float32),
            pltpu.VMEM((hp, MOBA_BLOCK, MOBA_BLOCK), jnp.bfloat16),
        ],
        compiler_params=pltpu.CompilerParams(
            dimension_semantics=("parallel", "parallel"), vmem_limit_bytes=VMEM_LIMIT),
        name="moba_attn",
    )(qT, k, vT, kmean)


MLP_TILE = 1024
FF_CHUNK = 1024


def _out_mlp_kernel(x_ref, aT_ref, cv_ref, woa_ref, woc_ref, g_ref, w1_ref, w2_ref, gf_ref, o_ref,
                    h_ref, *, final_norm):
    attn = aT_ref[0].T
    h = x_ref[...]
    h = h + jnp.dot(attn, woa_ref[...], preferred_element_type=jnp.float32)
    h = h + jnp.dot(cv_ref[...], woc_ref[...], preferred_element_type=jnp.float32)
    h_ref[...] = h
    hn = _rms(h, g_ref[...]).astype(jnp.bfloat16)

    def chunk(c, carry):
        cols = pl.ds(pl.multiple_of(c * FF_CHUNK, FF_CHUNK), FF_CHUNK)
        ff = jnp.maximum(jnp.dot(hn, w1_ref[:, cols], preferred_element_type=jnp.float32), 0.0)
        h_ref[...] += jnp.dot((ff * ff).astype(jnp.bfloat16), w2_ref[c], preferred_element_type=jnp.float32)
        return carry

    lax.fori_loop(0, D_FF // FF_CHUNK, chunk, 0)
    h = h_ref[...]
    o_ref[...] = _rms(h, gf_ref[...]) if final_norm else h


def _out_mlp(x2d, attnT, conv2d, woa, woc, g_mlp, w1, w2, g_final, final_norm):
    T = x2d.shape[0]
    S = attnT.shape[2]
    per = S // MLP_TILE
    const = lambda shape: pl.BlockSpec(shape, lambda t: (0,) * len(shape), pipeline_mode=pl.Buffered(1))
    return pl.pallas_call(
        functools.partial(_out_mlp_kernel, final_norm=final_norm),
        grid=(T // MLP_TILE,),
        in_specs=[
            pl.BlockSpec((MLP_TILE, D_MODEL), lambda t: (t, 0)),
            pl.BlockSpec((1, ATTN_WIDTH, MLP_TILE), lambda t: (t // per, 0, t % per)),
            pl.BlockSpec((MLP_TILE, CONV_WIDTH), lambda t: (t, 0)),
            const(woa.shape), const(woc.shape), const((1, D_MODEL)),
            const(w1.shape), const(w2.shape), const((1, D_MODEL)),
        ],
        out_specs=pl.BlockSpec((MLP_TILE, D_MODEL), lambda t: (t, 0)),
        out_shape=jax.ShapeDtypeStruct((T, D_MODEL), jnp.float32),
        scratch_shapes=[pltpu.VMEM((MLP_TILE, D_MODEL), jnp.float32)],
        compiler_params=pltpu.CompilerParams(
            dimension_semantics=("parallel",), vmem_limit_bytes=VMEM_LIMIT),
        name="out_mlp",
    )(x2d, attnT, conv2d, woa, woc, g_mlp, w1, w2, g_final)


def _rope_tables(S):
    inv_freq = ROPE_THETA ** (-jnp.arange(ROT_HALF, dtype=jnp.float32) * 2.0 / ROT_DIM)
    ang = jnp.arange(S, dtype=jnp.float32)[:, None] * inv_freq[None, :]
    cos, sin = jnp.cos(ang), jnp.sin(ang)
    zeros = jnp.zeros((S, K_LANES - ROT_DIM), jnp.float32)
    zero8 = jnp.zeros((S, ROT_HALF), jnp.float32)
    c = jnp.concatenate([cos, cos, jnp.ones_like(zeros)], axis=1)
    sa = jnp.concatenate([-sin, zero8, zeros], axis=1)
    sb = jnp.concatenate([zero8, sin, zeros], axis=1)
    return cos.T, sin.T, c, sa, sb


def _split_w_in(w, g):
    bf = jnp.bfloat16
    w = w * g[:, None]
    wq = w[:, :ATTN_WIDTH].T.reshape(N_HEADS, HEAD_DIM, D_MODEL) * (HEAD_DIM ** -0.5 * LOG2E)
    wq = jnp.pad(wq, ((0, 0), (0, Q_ROWS - HEAD_DIM), (0, 0))).reshape(N_HEADS * Q_ROWS, D_MODEL)
    wk = w[:, ATTN_WIDTH:2 * ATTN_WIDTH].reshape(D_MODEL, N_HEADS, HEAD_DIM)
    wk = jnp.pad(wk, ((0, 0), (0, 0), (0, K_LANES - HEAD_DIM))).reshape(D_MODEL, N_HEADS * K_LANES)
    wv = w[:, 2 * ATTN_WIDTH:3 * ATTN_WIDTH].T.reshape(N_HEADS, HEAD_DIM, D_MODEL)
    wv = jnp.pad(wv, ((0, 0), (0, V_ROWS - HEAD_DIM), (0, 0))).reshape(N_HEADS * V_ROWS, D_MODEL)
    wu = w[:, 3 * ATTN_WIDTH:].reshape(D_MODEL, 2, CONV_WIDTH // LANES, LANES)
    wu = wu.transpose(0, 2, 1, 3).reshape(D_MODEL, 2 * CONV_WIDTH)
    return wq.astype(bf), wk.astype(bf), wv.astype(bf), wu.astype(bf)


def _split_b_glu(b):
    return b.reshape(2, CONV_WIDTH // LANES, 1, LANES).transpose(1, 2, 0, 3).reshape(CONV_WIDTH // LANES, 1, 2 * LANES)


def kernel(x, g_mix_norm, w_in, b_glu, w_dw, b_dw, g_conv_ln, b_conv_ln, w_out, g_mlp_norm,
           w_mlp_in, w_mlp_out, g_final):
    B, S, D = x.shape
    depth = w_in.shape[0]
    bf = jnp.bfloat16
    tables = _rope_tables(S)
    h = x
    for l in range(depth):
        wq, wk, wv, wu = _split_w_in(w_in[l], g_mix_norm[l])
        qT, k, vT, kmean, conv = _in_proj(
            h, wq, wk, wv, wu, _split_b_glu(b_glu[l]), *tables,
            w_dw[l][:, 0, :], b_dw[l][None], g_conv_ln[l][None], b_conv_ln[l][None])
        attnT = _moba_attention(qT, k, vT, kmean)
        last = l == depth - 1
        h = _out_mlp(
            h.reshape(B * S, D), attnT, conv.reshape(B * S, CONV_WIDTH),
            w_out[l][:ATTN_WIDTH].astype(bf), w_out[l][ATTN_WIDTH:].astype(bf), g_mlp_norm[l][None],
            w_mlp_in[l].astype(bf),
            w_mlp_out[l].reshape(D_FF // FF_CHUNK, FF_CHUNK, D).astype(bf), g_final[None], final_norm=last,
        ).reshape(B, S, D)
    return h
```

```python
import functools

import jax
import jax.numpy as jnp
from jax import lax
from jax.experimental import pallas as pl
from jax.experimental.pallas import tpu as pltpu

D_MODEL = 1024
ATTN_WIDTH = 512
CONV_WIDTH = 512
HEAD_DIM = 64
N_HEADS = 8
ROT_DIM = 16
ROT_HALF = 8
ROPE_THETA = 500000.0
MOBA_BLOCK = 256
MOBA_TOP_K = 3
CONV_K = 31
D_FF = 4096
EPS = 1e-6

Q_ROWS = 128
K_LANES = 128
BIAS_ROW0 = HEAD_DIM
V_ROWS = 80
MASKED = -1e30
LOG2E = 1.4426950408889634

VMEM_LIMIT = 56 * 1024 * 1024

_NT = (((1,), (1,)), ((), ()))


def _rms(x, g):
    return x * lax.rsqrt(jnp.mean(x * x, axis=-1, keepdims=True) + EPS) * g


def _sigmoid(x):
    return 1.0 / (1.0 + jnp.exp(-x))


CONV_HALO = 32
CONV_ROWS = 64
LANES = 128
SUBLANES = 8


def _in_proj_kernel(x_ref, wq_ref, wk_ref, wv_ref, wu_ref, bglu_ref,
                    cosT_ref, sinT_ref, c_ref, sa_ref, sb_ref, wdw_ref, bdw_ref, gln_ref, bln_ref,
                    qT_ref, k_ref, vT_ref, kmean_ref, conv_ref, win_ref, y_ref):
    n = pl.program_id(1)

    @pl.when(n == 0)
    def _():
        win_ref[0, 0:CONV_HALO, :] = jnp.zeros((CONV_HALO, CONV_WIDTH), jnp.float32)

    x = x_ref[0]
    xb = (x * lax.rsqrt(jnp.mean(x * x, axis=-1, keepdims=True) + EPS)).astype(jnp.bfloat16)

    rows = CONV_HALO + MOBA_BLOCK
    off = CONV_HALO - (CONV_K - 1)
    for ch in range(CONV_WIDTH // LANES):
        lanes = slice(ch * LANES, (ch + 1) * LANES)
        u = jnp.dot(xb, wu_ref[:, 2 * ch * LANES:2 * (ch + 1) * LANES], preferred_element_type=jnp.float32) + bglu_ref[ch]
        win_ref[0, CONV_HALO:, lanes] = u[:, :LANES] * _sigmoid(u[:, LANES:])
        for b in range(1, SUBLANES):
            win_ref[b, 0:rows - SUBLANES, lanes] = win_ref[0, pl.ds(b, rows - SUBLANES), lanes]
        for r in range(MOBA_BLOCK // CONV_ROWS):
            acc = jnp.zeros((CONV_ROWS, LANES), jnp.float32)
            for j in range(CONV_K):
                a, b = divmod(off + j, SUBLANES)
                start = r * CONV_ROWS + a * SUBLANES
                acc = acc + win_ref[b, start:start + CONV_ROWS, lanes] * wdw_ref[j:j + 1, lanes]
            y_ref[r * CONV_ROWS:(r + 1) * CONV_ROWS, lanes] = acc
        win_ref[0, 0:CONV_HALO, lanes] = win_ref[0, MOBA_BLOCK:rows, lanes]

    qT = lax.dot_general(wq_ref[...], xb, _NT, preferred_element_type=jnp.float32)
    q3 = qT.reshape(N_HEADS, Q_ROWS, MOBA_BLOCK)
    cos = cosT_ref[...][None]
    sin = sinT_ref[...][None]
    x1 = q3[:, 0:ROT_HALF]
    x2 = q3[:, ROT_HALF:ROT_DIM]
    q3 = jnp.concatenate([x1 * cos - x2 * sin, x2 * cos + x1 * sin, q3[:, ROT_DIM:]], axis=1)
    qT_ref[0] = q3.reshape(N_HEADS * Q_ROWS, MOBA_BLOCK).astype(jnp.bfloat16)

    k = jnp.dot(xb, wk_ref[...], preferred_element_type=jnp.float32)
    c = c_ref[...]
    sa = sa_ref[...]
    sb = sb_ref[...]
    parts = []
    for h in range(N_HEADS):
        kh = k[:, h * K_LANES:(h + 1) * K_LANES]
        parts.append(kh * c + pltpu.roll(kh, K_LANES - ROT_HALF, 1) * sa + pltpu.roll(kh, ROT_HALF, 1) * sb)
    k = jnp.concatenate(parts, axis=1)
    kmean_ref[0, pl.ds(n, 1), :] = jnp.mean(k, axis=0, keepdims=True)
    lane = lax.broadcasted_iota(jnp.int32, (1, N_HEADS * K_LANES), 1) % K_LANES
    k_ref[0, 0] = (k + jnp.where(lane == BIAS_ROW0 + n, 1.0, 0.0)).astype(jnp.bfloat16)

    vT = lax.dot_general(wv_ref[...], xb, _NT, preferred_element_type=jnp.float32)
    vT_ref[0, 0] = vT.astype(jnp.bfloat16)
    first = lax.broadcasted_iota(jnp.int32, (V_ROWS - HEAD_DIM, MOBA_BLOCK), 0) == 0
    ones_rows = jnp.where(first, 1.0, 0.0).astype(jnp.bfloat16)
    for h in range(N_HEADS):
        vT_ref[0, 0, h * V_ROWS + HEAD_DIM:(h + 1) * V_ROWS, :] = ones_rows

    y = y_ref[...] + bdw_ref[...]
    mu = jnp.mean(y, axis=-1, keepdims=True)
    d = y - mu
    var = jnp.mean(d * d, axis=-1, keepdims=True)
    z = d * lax.rsqrt(var + EPS) * gln_ref[...] + bln_ref[...]
    conv_ref[0] = (z * _sigmoid(z)).astype(conv_ref.dtype)


def _in_proj(x, wq, wk, wv, wu, bglu, cosT, sinT, c, sa, sb, w_dw, b_dw, g_ln, b_ln):
    B, S, _ = x.shape
    nb = S // MOBA_BLOCK
    const = lambda shape: pl.BlockSpec(shape, lambda b, n: (0,) * len(shape))
    row = const((1, CONV_WIDTH))
    return pl.pallas_call(
        _in_proj_kernel,
        grid=(B, nb),
        in_specs=[
            pl.BlockSpec((1, MOBA_BLOCK, D_MODEL), lambda b, n: (b, n, 0)),
            const(wq.shape), const(wk.shape), const(wv.shape), const(wu.shape),
            const(bglu.shape),
            pl.BlockSpec((ROT_HALF, MOBA_BLOCK), lambda b, n: (0, n)),
            pl.BlockSpec((ROT_HALF, MOBA_BLOCK), lambda b, n: (0, n)),
            pl.BlockSpec((MOBA_BLOCK, K_LANES), lambda b, n: (n, 0)),
            pl.BlockSpec((MOBA_BLOCK, K_LANES), lambda b, n: (n, 0)),
            pl.BlockSpec((MOBA_BLOCK, K_LANES), lambda b, n: (n, 0)),
            const(w_dw.shape), row, row, row,
        ],
        out_specs=[
            pl.BlockSpec((1, N_HEADS * Q_ROWS, MOBA_BLOCK), lambda b, n: (b, 0, n)),
            pl.BlockSpec((1, 1, MOBA_BLOCK, N_HEADS * K_LANES), lambda b, n: (b, n, 0, 0)),
            pl.BlockSpec((1, 1, N_HEADS * V_ROWS, MOBA_BLOCK), lambda b, n: (b, n, 0, 0)),
            pl.BlockSpec((1, nb, N_HEADS * K_LANES), lambda b, n: (b, 0, 0)),
            pl.BlockSpec((1, MOBA_BLOCK, CONV_WIDTH), lambda b, n: (b, n, 0)),
        ],
        out_shape=[
            jax.ShapeDtypeStruct((B, N_HEADS * Q_ROWS, S), jnp.bfloat16),
            jax.ShapeDtypeStruct((B, nb, MOBA_BLOCK, N_HEADS * K_LANES), jnp.bfloat16),
            jax.ShapeDtypeStruct((B, nb, N_HEADS * V_ROWS, MOBA_BLOCK), jnp.bfloat16),
            jax.ShapeDtypeStruct((B, nb, N_HEADS * K_LANES), jnp.float32),
            jax.ShapeDtypeStruct((B, S, CONV_WIDTH), jnp.bfloat16),
        ],
        scratch_shapes=[
            pltpu.VMEM((SUBLANES, CONV_HALO + MOBA_BLOCK, CONV_WIDTH), jnp.float32),
            pltpu.VMEM((MOBA_BLOCK, CONV_WIDTH), jnp.float32),
        ],
        compiler_params=pltpu.CompilerParams(
            dimension_semantics=("parallel", "arbitrary"), vmem_limit_bytes=VMEM_LIMIT),
        name="in_proj",
    )(x, wq, wk, wv, wu, bglu, cosT, sinT, c, sa, sb, w_dw, b_dw, g_ln, b_ln)


HEADS_PER_STEP = 4


def _moba_kernel(qT_ref, k_ref, vT_ref, kmean_ref, o_ref, s_ref, p_ref):
    nb = k_ref.shape[1]
    kpos = lax.broadcasted_iota(jnp.int32, (MOBA_BLOCK, MOBA_BLOCK), 0)
    qpos = lax.broadcasted_iota(jnp.int32, (MOBA_BLOCK, MOBA_BLOCK), 1)
    causal = kpos <= qpos
    blk = lax.broadcasted_iota(jnp.int32, (nb, MOBA_BLOCK), 0)
    pad = jnp.zeros((Q_ROWS - HEAD_DIM - nb, MOBA_BLOCK), jnp.bfloat16)
    heads = range(HEADS_PER_STEP)

    def stage_values(h, n, p):
        vT_blk = vT_ref[0, n, h * V_ROWS:(h + 1) * V_ROWS, :]
        return jnp.dot(vT_blk, p, preferred_element_type=jnp.float32)

    def stage_softmax(s, m, own):
        if own:
            s = jnp.where(causal, s, -jnp.inf)
        m_new = jnp.maximum(m, jnp.max(s, axis=0, keepdims=True))
        return m_new, jnp.exp2(m - m_new), jnp.exp2(s - m_new).astype(jnp.bfloat16)

    def select(i):
        cols = pl.ds(pl.multiple_of(i * MOBA_BLOCK, MOBA_BLOCK), MOBA_BLOCK)
        qas = []
        for h in heads:
            qT = qT_ref[0, h * Q_ROWS:(h + 1) * Q_ROWS, cols]

            km = kmean_ref[0, :, h * K_LANES:(h + 1) * K_LANES]
            km_hi = km.astype(jnp.bfloat16)
            km_lo = (km - km_hi.astype(jnp.float32)).astype(jnp.bfloat16)
            g2 = jnp.dot(jnp.concatenate([km_hi, km_lo], axis=0), qT, preferred_element_type=jnp.float32)
            gate = g2[:nb] + g2[nb:]

            rem = jnp.where(blk < i, gate, -jnp.inf)
            sel = blk == i
            for _ in range(MOBA_TOP_K):
                mx = jnp.max(rem, axis=0, keepdims=True)
                first = jnp.min(jnp.where(rem == mx, blk, nb), axis=0, keepdims=True)
                pick = (blk == first) & (mx > -jnp.inf)
                sel = sel | pick
                rem = jnp.where(pick, -jnp.inf, rem)
            bias = jnp.where(sel, 0.0, MASKED).astype(jnp.bfloat16)

            qas.append(jnp.concatenate([qT[:HEAD_DIM], bias, pad], axis=0))
        return tuple(qas)

    def query_block(i, qas):
        cols = pl.ds(pl.multiple_of(i * MOBA_BLOCK, MOBA_BLOCK), MOBA_BLOCK)
        qas_next = select(jnp.minimum(i + 1, nb - 1))

        def stage_logits(h, n):
            k_blk = k_ref[0, n, :, h * K_LANES:(h + 1) * K_LANES]
            return jnp.dot(k_blk, qas[h], preferred_element_type=jnp.float32)

        for h in heads:
            s_ref[h] = stage_logits(h, 0)
            p_ref[h] = jnp.zeros((MOBA_BLOCK, MOBA_BLOCK), jnp.bfloat16)

        def values_behind(t, carry):
            pvs = [stage_values(h, jnp.maximum(t - 1, 0), p_ref[h]) for h in heads]
            return [acc * alpha + pv for (_, alpha, acc), pv in zip(carry, pvs)]

        def body(t, carry):
            ahead = [stage_logits(h, t + 1) for h in heads]
            accs = values_behind(t, carry)
            new = []
            for h in heads:
                m_new, alpha, p = stage_softmax(s_ref[h], carry[h][0], own=False)
                p_ref[h] = p
                new.append((m_new, alpha, accs[h]))
            for h in heads:
                s_ref[h] = ahead[h]
            return tuple(new)

        init = (jnp.full((1, MOBA_BLOCK), -jnp.inf, jnp.float32), jnp.ones((1, MOBA_BLOCK), jnp.float32),
                jnp.zeros((V_ROWS, MOBA_BLOCK), jnp.float32))
        carry = lax.fori_loop(0, i // 2, lambda tt, c: body(2 * tt + 1, body(2 * tt, c)),
                              (init,) * HEADS_PER_STEP)
        carry = lax.cond(i % 2 == 1, lambda c: body(i - 1, c), lambda c: c, carry)
        accs = values_behind(i, carry)
        for h in heads:
            _, alpha, p = stage_softmax(s_ref[h], carry[h][0], own=True)
            acc = accs[h] * alpha + stage_values(h, i, p)
            o_ref[0, h * HEAD_DIM:(h + 1) * HEAD_DIM, cols] = (
                acc[:HEAD_DIM] * (1.0 / acc[HEAD_DIM:HEAD_DIM + 1])).astype(o_ref.dtype)
        return qas_next

    lax.fori_loop(0, nb, query_block, select(0))


def _moba_attention(qT, k, vT, kmean):
    B, _, S = qT.shape
    nb = S // MOBA_BLOCK
    hp = HEADS_PER_STEP
    return pl.pallas_call(
        _moba_kernel,
        grid=(B, N_HEADS // hp),
        in_specs=[
            pl.BlockSpec((1, hp * Q_ROWS, S), lambda b, g: (b, g, 0)),
            pl.BlockSpec((1, nb, MOBA_BLOCK, hp * K_LANES), lambda b, g: (b, 0, 0, g)),
            pl.BlockSpec((1, nb, hp * V_ROWS, MOBA_BLOCK), lambda b, g: (b, 0, g, 0)),
            pl.BlockSpec((1, nb, hp * K_LANES), lambda b, g: (b, 0, g)),
        ],
        out_specs=pl.BlockSpec((1, hp * HEAD_DIM, S), lambda b, g: (b, g, 0)),
        out_shape=jax.ShapeDtypeStruct((B, ATTN_WIDTH, S), jnp.bfloat16),
        scratch_shapes=[
            pltpu.VMEM((hp, MOBA_BLOCK, MOBA_BLOCK), jnp.float32),
            pltpu.VMEM((hp, MOBA_BLOCK, MOBA_BLOCK), jnp.bfloat16),
        ],
        compiler_params=pltpu.CompilerParams(
            dimension_semantics=("parallel", "parallel"), vmem_limit_bytes=VMEM_LIMIT),
        name="moba_attn",
    )(qT, k, vT, kmean)


MLP_TILE = 1024
FF_CHUNK = 1024


def _out_mlp_kernel(x_ref, aT_ref, cv_ref, woa_ref, woc_ref, g_ref, w1_ref, w2_ref, gf_ref, o_ref,
                    h_ref, *, final_norm):
    attn = aT_ref[0].T
    h = x_ref[...]
    h = h + jnp.dot(attn, woa_ref[...], preferred_element_type=jnp.float32)
    h = h + jnp.dot(cv_ref[...], woc_ref[...], preferred_element_type=jnp.float32)
    h_ref[...] = h
    hn = _rms(h, g_ref[...]).astype(jnp.bfloat16)

    def chunk(c, carry):
        cols = pl.ds(pl.multiple_of(c * FF_CHUNK, FF_CHUNK), FF_CHUNK)
        ff = jnp.maximum(jnp.dot(hn, w1_ref[:, cols], preferred_element_type=jnp.float32), 0.0)
        h_ref[...] += jnp.dot((ff * ff).astype(jnp.bfloat16), w2_ref[c], preferred_element_type=jnp.float32)
        return carry

    lax.fori_loop(0, D_FF // FF_CHUNK, chunk, 0)
    h = h_ref[...]
    o_ref[...] = _rms(h, gf_ref[...]) if final_norm else h


def _out_mlp(x2d, attnT, conv2d, woa, woc, g_mlp, w1, w2, g_final, final_norm):
    T = x2d.shape[0]
    S = attnT.shape[2]
    per = S // MLP_TILE
    const = lambda shape: pl.BlockSpec(shape, lambda t: (0,) * len(shape), pipeline_mode=pl.Buffered(1))
    return pl.pallas_call(
        functools.partial(_out_mlp_kernel, final_norm=final_norm),
        grid=(T // MLP_TILE,),
        in_specs=[
            pl.BlockSpec((MLP_TILE, D_MODEL), lambda t: (t, 0)),
            pl.BlockSpec((1, ATTN_WIDTH, MLP_TILE), lambda t: (t // per, 0, t % per)),
            pl.BlockSpec((MLP_TILE, CONV_WIDTH), lambda t: (t, 0)),
            const(woa.shape), const(woc.shape), const((1, D_MODEL)),
            const(w1.shape), const(w2.shape), const((1, D_MODEL)),
        ],
        out_specs=pl.BlockSpec((MLP_TILE, D_MODEL), lambda t: (t, 0)),
        out_shape=jax.ShapeDtypeStruct((T, D_MODEL), jnp.float32),
        scratch_shapes=[pltpu.VMEM((MLP_TILE, D_MODEL), jnp.float32)],
        compiler_params=pltpu.CompilerParams(
            dimension_semantics=("parallel",), vmem_limit_bytes=VMEM_LIMIT),
        name="out_mlp",
    )(x2d, attnT, conv2d, woa, woc, g_mlp, w1, w2, g_final)


def _rope_tables(S):
    inv_freq = ROPE_THETA ** (-jnp.arange(ROT_HALF, dtype=jnp.float32) * 2.0 / ROT_DIM)
    ang = jnp.arange(S, dtype=jnp.float32)[:, None] * inv_freq[None, :]
    cos, sin = jnp.cos(ang), jnp.sin(ang)
    zeros = jnp.zeros((S, K_LANES - ROT_DIM), jnp.float32)
    zero8 = jnp.zeros((S, ROT_HALF), jnp.float32)
    c = jnp.concatenate([cos, cos, jnp.ones_like(zeros)], axis=1)
    sa = jnp.concatenate([-sin, zero8, zeros], axis=1)
    sb = jnp.concatenate([zero8, sin, zeros], axis=1)
    return cos.T, sin.T, c, sa, sb


def _split_w_in(w, g):
    bf = jnp.bfloat16
    w = w * g[:, None]
    wq = w[:, :ATTN_WIDTH].T.reshape(N_HEADS, HEAD_DIM, D_MODEL) * (HEAD_DIM ** -0.5 * LOG2E)
    wq = jnp.pad(wq, ((0, 0), (0, Q_ROWS - HEAD_DIM), (0, 0))).reshape(N_HEADS * Q_ROWS, D_MODEL)
    wk = w[:, ATTN_WIDTH:2 * ATTN_WIDTH].reshape(D_MODEL, N_HEADS, HEAD_DIM)
    wk = jnp.pad(wk, ((0, 0), (0, 0), (0, K_LANES - HEAD_DIM))).reshape(D_MODEL, N_HEADS * K_LANES)
    wv = w[:, 2 * ATTN_WIDTH:3 * ATTN_WIDTH].T.reshape(N_HEADS, HEAD_DIM, D_MODEL)
    wv = jnp.pad(wv, ((0, 0), (0, V_ROWS - HEAD_DIM), (0, 0))).reshape(N_HEADS * V_ROWS, D_MODEL)
    wu = w[:, 3 * ATTN_WIDTH:].reshape(D_MODEL, 2, CONV_WIDTH // LANES, LANES)
    wu = wu.transpose(0, 2, 1, 3).reshape(D_MODEL, 2 * CONV_WIDTH)
    return wq.astype(bf), wk.astype(bf), wv.astype(bf), wu.astype(bf)


def _split_b_glu(b):
    return b.reshape(2, CONV_WIDTH // LANES, 1, LANES).transpose(1, 2, 0, 3).reshape(CONV_WIDTH // LANES, 1, 2 * LANES)


def kernel(x, g_mix_norm, w_in, b_glu, w_dw, b_dw, g_conv_ln, b_conv_ln, w_out, g_mlp_norm,
           w_mlp_in, w_mlp_out, g_final):
    B, S, D = x.shape
    depth = w_in.shape[0]
    bf = jnp.bfloat16
    tables = _rope_tables(S)
    h = x
    for l in range(depth):
        wq, wk, wv, wu = _split_w_in(w_in[l], g_mix_norm[l])
        qT, k, vT, kmean, conv = _in_proj(
            h, wq, wk, wv, wu, _split_b_glu(b_glu[l]), *tables,
            w_dw[l][:, 0, :], b_dw[l][None], g_conv_ln[l][None], b_conv_ln[l][None])
        attnT = _moba_attention(qT, k, vT, kmean)
        last = l == depth - 1
        h = _out_mlp(
            h.reshape(B * S, D), attnT, conv.reshape(B * S, CONV_WIDTH),
            w_out[l][:ATTN_WIDTH].astype(bf), w_out[l][ATTN_WIDTH:].astype(bf), g_mlp_norm[l][None],
            w_mlp_in[l].astype(bf),
            w_mlp_out[l].reshape(D_FF // FF_CHUNK, FF_CHUNK, D).astype(bf), g_final[None], final_norm=last,
        ).reshape(B, S, D)
    return h
```

```python
import functools

import jax
import jax.numpy as jnp
from jax import lax
from jax.experimental import pallas as pl
from jax.experimental.pallas import tpu as pltpu

D_MODEL = 1024
ATTN_WIDTH = 512
CONV_WIDTH = 512
HEAD_DIM = 64
N_HEADS = 8
ROT_DIM = 16
ROT_HALF = 8
ROPE_THETA = 500000.0
MOBA_BLOCK = 256
MOBA_TOP_K = 3
CONV_K = 31
D_FF = 4096
EPS = 1e-6

Q_ROWS = 128
K_LANES = 128
BIAS_ROW0 = HEAD_DIM
V_ROWS = 80
MASKED = -1e30
LOG2E = 1.4426950408889634

VMEM_LIMIT = 56 * 1024 * 1024

_NT = (((1,), (1,)), ((), ()))


def _rms(x, g):
    return x * lax.rsqrt(jnp.mean(x * x, axis=-1, keepdims=True) + EPS) * g


def _sigmoid(x):
    return 1.0 / (1.0 + jnp.exp(-x))


CONV_HALO = 32
CONV_ROWS = 64
LANES = 128
SUBLANES = 8


PROJ_BLOCKS = 2
PROJ_TILE = PROJ_BLOCKS * MOBA_BLOCK


def _in_proj_kernel(x_ref, wq_ref, wk_ref, wv_ref, wu_ref, bglu_ref,
                    cosT_ref, sinT_ref, c_ref, sa_ref, sb_ref, wdw_ref, bdw_ref, gln_ref, bln_ref,
                    qT_ref, k_ref, vT_ref, kmean_ref, conv_ref, win_ref, y_ref):
    n = pl.program_id(1)

    @pl.when(n == 0)
    def _():
        win_ref[0, 0:CONV_HALO, :] = jnp.zeros((CONV_HALO, CONV_WIDTH), jnp.float32)

    x = x_ref[0]
    xb = (x * lax.rsqrt(jnp.mean(x * x, axis=-1, keepdims=True) + EPS)).astype(jnp.bfloat16)

    rows = CONV_HALO + PROJ_TILE
    off = CONV_HALO - (CONV_K - 1)
    for ch in range(CONV_WIDTH // LANES):
        lanes = slice(ch * LANES, (ch + 1) * LANES)
        u = jnp.dot(xb, wu_ref[:, 2 * ch * LANES:2 * (ch + 1) * LANES], preferred_element_type=jnp.float32) + bglu_ref[ch]
        win_ref[0, CONV_HALO:, lanes] = u[:, :LANES] * _sigmoid(u[:, LANES:])
        for b in range(1, SUBLANES):
            win_ref[b, 0:rows - SUBLANES, lanes] = win_ref[0, pl.ds(b, rows - SUBLANES), lanes]
        for r in range(PROJ_TILE // CONV_ROWS):
            acc = jnp.zeros((CONV_ROWS, LANES), jnp.float32)
            for j in range(CONV_K):
                a, b = divmod(off + j, SUBLANES)
                start = r * CONV_ROWS + a * SUBLANES
                acc = acc + win_ref[b, start:start + CONV_ROWS, lanes] * wdw_ref[j:j + 1, lanes]
            y_ref[r * CONV_ROWS:(r + 1) * CONV_ROWS, lanes] = acc
        win_ref[0, 0:CONV_HALO, lanes] = win_ref[0, PROJ_TILE:rows, lanes]

    qT = lax.dot_general(wq_ref[...], xb, _NT, preferred_element_type=jnp.float32)
    q3 = qT.reshape(N_HEADS, Q_ROWS, PROJ_TILE)
    cos = cosT_ref[...][None]
    sin = sinT_ref[...][None]
    x1 = q3[:, 0:ROT_HALF]
    x2 = q3[:, ROT_HALF:ROT_DIM]
    q3 = jnp.concatenate([x1 * cos - x2 * sin, x2 * cos + x1 * sin, q3[:, ROT_DIM:]], axis=1)
    qT_ref[0] = q3.reshape(N_HEADS * Q_ROWS, PROJ_TILE).astype(jnp.bfloat16)

    k = jnp.dot(xb, wk_ref[...], preferred_element_type=jnp.float32)
    c = c_ref[...]
    sa = sa_ref[...]
    sb = sb_ref[...]
    parts = []
    for h in range(N_HEADS):
        kh = k[:, h * K_LANES:(h + 1) * K_LANES]
        parts.append(kh * c + pltpu.roll(kh, K_LANES - ROT_HALF, 1) * sa + pltpu.roll(kh, ROT_HALF, 1) * sb)
    k = jnp.concatenate(parts, axis=1)
    lane = lax.broadcasted_iota(jnp.int32, (1, N_HEADS * K_LANES), 1) % K_LANES
    for j in range(PROJ_BLOCKS):
        blk = PROJ_BLOCKS * n + j
        kj = k[j * MOBA_BLOCK:(j + 1) * MOBA_BLOCK]
        kmean_ref[0, pl.ds(blk, 1), :] = jnp.mean(kj, axis=0, keepdims=True)
        k_ref[0, j] = (kj + jnp.where(lane == BIAS_ROW0 + blk, 1.0, 0.0)).astype(jnp.bfloat16)

    vT = lax.dot_general(wv_ref[...], xb, _NT, preferred_element_type=jnp.float32).astype(jnp.bfloat16)
    first = lax.broadcasted_iota(jnp.int32, (V_ROWS - HEAD_DIM, MOBA_BLOCK), 0) == 0
    ones_rows = jnp.where(first, 1.0, 0.0).astype(jnp.bfloat16)
    for j in range(PROJ_BLOCKS):
        vT_ref[0, j] = vT[:, j * MOBA_BLOCK:(j + 1) * MOBA_BLOCK]
        for h in range(N_HEADS):
            vT_ref[0, j, h * V_ROWS + HEAD_DIM:(h + 1) * V_ROWS, :] = ones_rows

    y = y_ref[...] + bdw_ref[...]
    mu = jnp.mean(y, axis=-1, keepdims=True)
    d = y - mu
    var = jnp.mean(d * d, axis=-1, keepdims=True)
    z = d * lax.rsqrt(var + EPS) * gln_ref[...] + bln_ref[...]
    conv_ref[0] = (z * _sigmoid(z)).astype(conv_ref.dtype)


def _in_proj(x, wq, wk, wv, wu, bglu, cosT, sinT, c, sa, sb, w_dw, b_dw, g_ln, b_ln):
    B, S, _ = x.shape
    nb = S // MOBA_BLOCK
    const = lambda shape: pl.BlockSpec(shape, lambda b, n: (0,) * len(shape))
    row = const((1, CONV_WIDTH))
    return pl.pallas_call(
        _in_proj_kernel,
        grid=(B, S // PROJ_TILE),
        in_specs=[
            pl.BlockSpec((1, PROJ_TILE, D_MODEL), lambda b, n: (b, n, 0)),
            const(wq.shape), const(wk.shape), const(wv.shape), const(wu.shape),
            const(bglu.shape),
            pl.BlockSpec((ROT_HALF, PROJ_TILE), lambda b, n: (0, n)),
            pl.BlockSpec((ROT_HALF, PROJ_TILE), lambda b, n: (0, n)),
            pl.BlockSpec((PROJ_TILE, K_LANES), lambda b, n: (n, 0)),
            pl.BlockSpec((PROJ_TILE, K_LANES), lambda b, n: (n, 0)),
            pl.BlockSpec((PROJ_TILE, K_LANES), lambda b, n: (n, 0)),
            const(w_dw.shape), row, row, row,
        ],
        out_specs=[
            pl.BlockSpec((1, N_HEADS * Q_ROWS, PROJ_TILE), lambda b, n: (b, 0, n)),
            pl.BlockSpec((1, PROJ_BLOCKS, MOBA_BLOCK, N_HEADS * K_LANES), lambda b, n: (b, n, 0, 0)),
            pl.BlockSpec((1, PROJ_BLOCKS, N_HEADS * V_ROWS, MOBA_BLOCK), lambda b, n: (b, n, 0, 0)),
            pl.BlockSpec((1, nb, N_HEADS * K_LANES), lambda b, n: (b, 0, 0)),
            pl.BlockSpec((1, PROJ_TILE, CONV_WIDTH), lambda b, n: (b, n, 0)),
        ],
        out_shape=[
            jax.ShapeDtypeStruct((B, N_HEADS * Q_ROWS, S), jnp.bfloat16),
            jax.ShapeDtypeStruct((B, nb, MOBA_BLOCK, N_HEADS * K_LANES), jnp.bfloat16),
            jax.ShapeDtypeStruct((B, nb, N_HEADS * V_ROWS, MOBA_BLOCK), jnp.bfloat16),
            jax.ShapeDtypeStruct((B, nb, N_HEADS * K_LANES), jnp.float32),
            jax.ShapeDtypeStruct((B, S, CONV_WIDTH), jnp.bfloat16),
        ],
        scratch_shapes=[
            pltpu.VMEM((SUBLANES, CONV_HALO + PROJ_TILE, CONV_WIDTH), jnp.float32),
            pltpu.VMEM((PROJ_TILE, CONV_WIDTH), jnp.float32),
        ],
        compiler_params=pltpu.CompilerParams(
            dimension_semantics=("parallel", "arbitrary"), vmem_limit_bytes=VMEM_LIMIT),
        name="in_proj",
    )(x, wq, wk, wv, wu, bglu, cosT, sinT, c, sa, sb, w_dw, b_dw, g_ln, b_ln)


HEADS_PER_STEP = 4


def _moba_kernel(qT_ref, k_ref, vT_ref, kmean_ref, o_ref, s_ref, p_ref):
    nb = k_ref.shape[1]
    kpos = lax.broadcasted_iota(jnp.int32, (MOBA_BLOCK, MOBA_BLOCK), 0)
    qpos = lax.broadcasted_iota(jnp.int32, (MOBA_BLOCK, MOBA_BLOCK), 1)
    causal = kpos <= qpos
    blk = lax.broadcasted_iota(jnp.int32, (nb, MOBA_BLOCK), 0)
    pad = jnp.zeros((Q_ROWS - HEAD_DIM - nb, MOBA_BLOCK), jnp.bfloat16)
    heads = range(HEADS_PER_STEP)

    def stage_values(h, n, p):
        vT_blk = vT_ref[0, n, h * V_ROWS:(h + 1) * V_ROWS, :]
        return jnp.dot(vT_blk, p, preferred_element_type=jnp.float32)

    def stage_softmax(s, m, own):
        if own:
            s = jnp.where(causal, s, -jnp.inf)
        m_new = jnp.maximum(m, jnp.max(s, axis=0, keepdims=True))
        return m_new, jnp.exp2(m - m_new), jnp.exp2(s - m_new).astype(jnp.bfloat16)

    def select(i):
        cols = pl.ds(pl.multiple_of(i * MOBA_BLOCK, MOBA_BLOCK), MOBA_BLOCK)
        qas = []
        for h in heads:
            qT = qT_ref[0, h * Q_ROWS:(h + 1) * Q_ROWS, cols]

            km = kmean_ref[0, :, h * K_LANES:(h + 1) * K_LANES]
            km_hi = km.astype(jnp.bfloat16)
            km_lo = (km - km_hi.astype(jnp.float32)).astype(jnp.bfloat16)
            g2 = jnp.dot(jnp.concatenate([km_hi, km_lo], axis=0), qT, preferred_element_type=jnp.float32)
            gate = g2[:nb] + g2[nb:]

            rem = jnp.where(blk < i, gate, -jnp.inf)
            sel = blk == i
            for _ in range(MOBA_TOP_K):
                mx = jnp.max(rem, axis=0, keepdims=True)
                first = jnp.min(jnp.where(rem == mx, blk, nb), axis=0, keepdims=True)
                pick = (blk == first) & (mx > -jnp.inf)
                sel = sel | pick
                rem = jnp.where(pick, -jnp.inf, rem)
            bias = jnp.where(sel, 0.0, MASKED).astype(jnp.bfloat16)

            qas.append(jnp.concatenate([qT[:HEAD_DIM], bias, pad], axis=0))
        return tuple(qas)

    def query_block(i, qas):
        cols = pl.ds(pl.multiple_of(i * MOBA_BLOCK, MOBA_BLOCK), MOBA_BLOCK)
        qas_next = select(jnp.minimum(i + 1, nb - 1))

        def stage_logits(h, n):
            k_blk = k_ref[0, n, :, h * K_LANES:(h + 1) * K_LANES]
            return jnp.dot(k_blk, qas[h], preferred_element_type=jnp.float32)

        for h in heads:
            s_ref[h] = stage_logits(h, 0)
            p_ref[h] = jnp.zeros((MOBA_BLOCK, MOBA_BLOCK), jnp.bfloat16)

        def values_behind(t, carry):
            pvs = [stage_values(h, jnp.maximum(t - 1, 0), p_ref[h]) for h in heads]
            return [acc * alpha + pv for (_, alpha, acc), pv in zip(carry, pvs)]

        def body(t, carry):
            ahead = [stage_logits(h, t + 1) for h in heads]
            accs = values_behind(t, carry)
            new = []
            for h in heads:
                m_new, alpha, p = stage_softmax(s_ref[h], carry[h][0], own=False)
                p_ref[h] = p
                new.append((m_new, alpha, accs[h]))
            for h in heads:
                s_ref[h] = ahead[h]
            return tuple(new)

        init = (jnp.full((1, MOBA_BLOCK), -jnp.inf, jnp.float32), jnp.ones((1, MOBA_BLOCK), jnp.float32),
                jnp.zeros((V_ROWS, MOBA_BLOCK), jnp.float32))
        carry = lax.fori_loop(0, i // 2, lambda tt, c: body(2 * tt + 1, body(2 * tt, c)),
                              (init,) * HEADS_PER_STEP)
        carry = lax.cond(i % 2 == 1, lambda c: body(i - 1, c), lambda c: c, carry)
        accs = values_behind(i, carry)
        for h in heads:
            _, alpha, p = stage_softmax(s_ref[h], carry[h][0], own=True)
            acc = accs[h] * alpha + stage_values(h, i, p)
            o_ref[0, h * HEAD_DIM:(h + 1) * HEAD_DIM, cols] = (
                acc[:HEAD_DIM] * (1.0 / acc[HEAD_DIM:HEAD_DIM + 1])).astype(o_ref.dtype)
        return qas_next

    lax.fori_loop(0, nb, query_block, select(0))


def _moba_attention(qT, k, vT, kmean):
    B, _, S = qT.shape
    nb = S // MOBA_BLOCK
    hp = HEADS_PER_STEP
    return pl.pallas_call(
        _moba_kernel,
        grid=(B, N_HEADS // hp),
        in_specs=[
            pl.BlockSpec((1, hp * Q_ROWS, S), lambda b, g: (b, g, 0)),
            pl.BlockSpec((1, nb, MOBA_BLOCK, hp * K_LANES), lambda b, g: (b, 0, 0, g)),
            pl.BlockSpec((1, nb, hp * V_ROWS, MOBA_BLOCK), lambda b, g: (b, 0, g, 0)),
            pl.BlockSpec((1, nb, hp * K_LANES), lambda b, g: (b, 0, g)),
        ],
        out_specs=pl.BlockSpec((1, hp * HEAD_DIM, S), lambda b, g: (b, g, 0)),
        out_shape=jax.ShapeDtypeStruct((B, ATTN_WIDTH, S), jnp.bfloat16),
        scratch_shapes=[
            pltpu.VMEM((hp, MOBA_BLOCK, MOBA_BLOCK), jnp.float32),
            pltpu.VMEM((hp, MOBA_BLOCK, MOBA_BLOCK), jnp.bfloat16),
        ],
        compiler_params=pltpu.CompilerParams(
            dimension_semantics=("parallel", "parallel"), vmem_limit_bytes=VMEM_LIMIT),
        name="moba_attn",
    )(qT, k, vT, kmean)


MLP_TILE = 1024
FF_CHUNK = 1024


def _out_mlp_kernel(x_ref, aT_ref, cv_ref, woa_ref, woc_ref, g_ref, w1_ref, w2_ref, gf_ref, o_ref,
                    h_ref, *, final_norm):
    attn = aT_ref[0].T
    h = x_ref[...]
    h = h + jnp.dot(attn, woa_ref[...], preferred_element_type=jnp.float32)
    h = h + jnp.dot(cv_ref[...], woc_ref[...], preferred_element_type=jnp.float32)
    h_ref[...] = h
    hn = _rms(h, g_ref[...]).astype(jnp.bfloat16)

    def chunk(c, carry):
        cols = pl.ds(pl.multiple_of(c * FF_CHUNK, FF_CHUNK), FF_CHUNK)
        ff = jnp.maximum(jnp.dot(hn, w1_ref[:, cols], preferred_element_type=jnp.float32), 0.0)
        h_ref[...] += jnp.dot((ff * ff).astype(jnp.bfloat16), w2_ref[c], preferred_element_type=jnp.float32)
        return carry

    lax.fori_loop(0, D_FF // FF_CHUNK, chunk, 0)
    h = h_ref[...]
    o_ref[...] = _rms(h, gf_ref[...]) if final_norm else h


def _out_mlp(x2d, attnT, conv2d, woa, woc, g_mlp, w1, w2, g_final, final_norm):
    T = x2d.shape[0]
    S = attnT.shape[2]
    per = S // MLP_TILE
    const = lambda shape: pl.BlockSpec(shape, lambda t: (0,) * len(shape), pipeline_mode=pl.Buffered(1))
    return pl.pallas_call(
        functools.partial(_out_mlp_kernel, final_norm=final_norm),
        grid=(T // MLP_TILE,),
        in_specs=[
            pl.BlockSpec((MLP_TILE, D_MODEL), lambda t: (t, 0)),
            pl.BlockSpec((1, ATTN_WIDTH, MLP_TILE), lambda t: (t // per, 0, t % per)),
            pl.BlockSpec((MLP_TILE, CONV_WIDTH), lambda t: (t, 0)),
            const(woa.shape), const(woc.shape), const((1, D_MODEL)),
            const(w1.shape), const(w2.shape), const((1, D_MODEL)),
        ],
        out_specs=pl.BlockSpec((MLP_TILE, D_MODEL), lambda t: (t, 0)),
        out_shape=jax.ShapeDtypeStruct((T, D_MODEL), jnp.float32),
        scratch_shapes=[pltpu.VMEM((MLP_TILE, D_MODEL), jnp.float32)],
        compiler_params=pltpu.CompilerParams(
            dimension_semantics=("parallel",), vmem_limit_bytes=VMEM_LIMIT),
        name="out_mlp",
    )(x2d, attnT, conv2d, woa, woc, g_mlp, w1, w2, g_final)


def _rope_tables(S):
    inv_freq = ROPE_THETA ** (-jnp.arange(ROT_HALF, dtype=jnp.float32) * 2.0 / ROT_DIM)
    ang = jnp.arange(S, dtype=jnp.float32)[:, None] * inv_freq[None, :]
    cos, sin = jnp.cos(ang), jnp.sin(ang)
    zeros = jnp.zeros((S, K_LANES - ROT_DIM), jnp.float32)
    zero8 = jnp.zeros((S, ROT_HALF), jnp.float32)
    c = jnp.concatenate([cos, cos, jnp.ones_like(zeros)], axis=1)
    sa = jnp.concatenate([-sin, zero8, zeros], axis=1)
    sb = jnp.concatenate([zero8, sin, zeros], axis=1)
    return cos.T, sin.T, c, sa, sb


def _split_w_in(w, g):
    bf = jnp.bfloat16
    w = w * g[:, None]
    wq = w[:, :ATTN_WIDTH].T.reshape(N_HEADS, HEAD_DIM, D_MODEL) * (HEAD_DIM ** -0.5 * LOG2E)
    wq = jnp.pad(wq, ((0, 0), (0, Q_ROWS - HEAD_DIM), (0, 0))).reshape(N_HEADS * Q_ROWS, D_MODEL)
    wk = w[:, ATTN_WIDTH:2 * ATTN_WIDTH].reshape(D_MODEL, N_HEADS, HEAD_DIM)
    wk = jnp.pad(wk, ((0, 0), (0, 0), (0, K_LANES - HEAD_DIM))).reshape(D_MODEL, N_HEADS * K_LANES)
    wv = w[:, 2 * ATTN_WIDTH:3 * ATTN_WIDTH].T.reshape(N_HEADS, HEAD_DIM, D_MODEL)
    wv = jnp.pad(wv, ((0, 0), (0, V_ROWS - HEAD_DIM), (0, 0))).reshape(N_HEADS * V_ROWS, D_MODEL)
    wu = w[:, 3 * ATTN_WIDTH:].reshape(D_MODEL, 2, CONV_WIDTH // LANES, LANES)
    wu = wu.transpose(0, 2, 1, 3).reshape(D_MODEL, 2 * CONV_WIDTH)
    return wq.astype(bf), wk.astype(bf), wv.astype(bf), wu.astype(bf)


def _split_b_glu(b):
    return b.reshape(2, CONV_WIDTH // LANES, 1, LANES).transpose(1, 2, 0, 3).reshape(CONV_WIDTH // LANES, 1, 2 * LANES)


def kernel(x, g_mix_norm, w_in, b_glu, w_dw, b_dw, g_conv_ln, b_conv_ln, w_out, g_mlp_norm,
           w_mlp_in, w_mlp_out, g_final):
    B, S, D = x.shape
    depth = w_in.shape[0]
    bf = jnp.bfloat16
    tables = _rope_tables(S)
    h = x
    for l in range(depth):
        wq, wk, wv, wu = _split_w_in(w_in[l], g_mix_norm[l])
        qT, k, vT, kmean, conv = _in_proj(
            h, wq, wk, wv, wu, _split_b_glu(b_glu[l]), *tables,
            w_dw[l][:, 0, :], b_dw[l][None], g_conv_ln[l][None], b_conv_ln[l][None])
        attnT = _moba_attention(qT, k, vT, kmean)
        last = l == depth - 1
        h = _out_mlp(
            h.reshape(B * S, D), attnT, conv.reshape(B * S, CONV_WIDTH),
            w_out[l][:ATTN_WIDTH].astype(bf), w_out[l][ATTN_WIDTH:].astype(bf), g_mlp_norm[l][None],
            w_mlp_in[l].astype(bf),
            w_mlp_out[l].reshape(D_FF // FF_CHUNK, FF_CHUNK, D).astype(bf), g_final[None], final_norm=last,
        ).reshape(B, S, D)
    return h
```

```python
import functools

import jax
import jax.numpy as jnp
from jax import lax
from jax.experimental import pallas as pl
from jax.experimental.pallas import tpu as pltpu

D_MODEL = 1024
ATTN_WIDTH = 512
CONV_WIDTH = 512
HEAD_DIM = 64
N_HEADS = 8
ROT_DIM = 16
ROT_HALF = 8
ROPE_THETA = 500000.0
MOBA_BLOCK = 256
MOBA_TOP_K = 3
CONV_K = 31
D_FF = 4096
EPS = 1e-6

Q_ROWS = 128
K_LANES = 128
BIAS_ROW0 = HEAD_DIM
V_ROWS = 80
MASKED = -1e30
LOG2E = 1.4426950408889634

VMEM_LIMIT = 56 * 1024 * 1024

_NT = (((1,), (1,)), ((), ()))


def _rms(x, g):
    return x * lax.rsqrt(jnp.mean(x * x, axis=-1, keepdims=True) + EPS) * g


def _sigmoid(x):
    return 1.0 / (1.0 + jnp.exp(-x))


CONV_HALO = 32
CONV_ROWS = 64
LANES = 128
SUBLANES = 8


PROJ_BLOCKS = 2
PROJ_TILE = PROJ_BLOCKS * MOBA_BLOCK


def _in_proj_kernel(x_ref, wq_ref, wk_ref, wv_ref, wu_ref, bglu_ref,
                    cosT_ref, sinT_ref, c_ref, sa_ref, sb_ref, wdw_ref, bdw_ref, gln_ref, bln_ref,
                    qT_ref, k_ref, vT_ref, kmean_ref, conv_ref, win_ref, y_ref):
    n = pl.program_id(1)

    @pl.when(n == 0)
    def _():
        win_ref[0, 0:CONV_HALO, :] = jnp.zeros((CONV_HALO, CONV_WIDTH), jnp.float32)

    x = x_ref[0]
    xb = (x * lax.rsqrt(jnp.mean(x * x, axis=-1, keepdims=True) + EPS)).astype(jnp.bfloat16)

    rows = CONV_HALO + PROJ_TILE
    off = CONV_HALO - (CONV_K - 1)
    for ch in range(CONV_WIDTH // LANES):
        lanes = slice(ch * LANES, (ch + 1) * LANES)
        u = jnp.dot(xb, wu_ref[:, 2 * ch * LANES:2 * (ch + 1) * LANES], preferred_element_type=jnp.float32) + bglu_ref[ch]
        win_ref[0, CONV_HALO:, lanes] = u[:, :LANES] * _sigmoid(u[:, LANES:])
        for b in range(1, SUBLANES):
            win_ref[b, 0:rows - SUBLANES, lanes] = win_ref[0, pl.ds(b, rows - SUBLANES), lanes]
        for r in range(PROJ_TILE // CONV_ROWS):
            acc = jnp.zeros((CONV_ROWS, LANES), jnp.float32)
            for j in range(CONV_K):
                a, b = divmod(off + j, SUBLANES)
                start = r * CONV_ROWS + a * SUBLANES
                acc = acc + win_ref[b, start:start + CONV_ROWS, lanes] * wdw_ref[j:j + 1, lanes]
            y_ref[r * CONV_ROWS:(r + 1) * CONV_ROWS, lanes] = acc
        win_ref[0, 0:CONV_HALO, lanes] = win_ref[0, PROJ_TILE:rows, lanes]

    qT = lax.dot_general(wq_ref[...], xb, _NT, preferred_element_type=jnp.float32)
    q3 = qT.reshape(N_HEADS, Q_ROWS, PROJ_TILE)
    cos = cosT_ref[...][None]
    sin = sinT_ref[...][None]
    x1 = q3[:, 0:ROT_HALF]
    x2 = q3[:, ROT_HALF:ROT_DIM]
    q3 = jnp.concatenate([x1 * cos - x2 * sin, x2 * cos + x1 * sin, q3[:, ROT_DIM:]], axis=1)
    qT_ref[0] = q3.reshape(N_HEADS * Q_ROWS, PROJ_TILE).astype(jnp.bfloat16)

    k = jnp.dot(xb, wk_ref[...], preferred_element_type=jnp.float32)
    c = c_ref[...]
    sa = sa_ref[...]
    sb = sb_ref[...]
    parts = []
    for h in range(N_HEADS):
        kh = k[:, h * K_LANES:(h + 1) * K_LANES]
        parts.append(kh * c + pltpu.roll(kh, K_LANES - ROT_HALF, 1) * sa + pltpu.roll(kh, ROT_HALF, 1) * sb)
    k = jnp.concatenate(parts, axis=1)
    lane = lax.broadcasted_iota(jnp.int32, (1, N_HEADS * K_LANES), 1) % K_LANES
    for j in range(PROJ_BLOCKS):
        blk = PROJ_BLOCKS * n + j
        kj = k[j * MOBA_BLOCK:(j + 1) * MOBA_BLOCK]
        kmean_ref[0, pl.ds(blk, 1), :] = jnp.mean(kj, axis=0, keepdims=True)
        k_ref[0, j] = (kj + jnp.where(lane == BIAS_ROW0 + blk, 1.0, 0.0)).astype(jnp.bfloat16)

    vT = lax.dot_general(wv_ref[...], xb, _NT, preferred_element_type=jnp.float32).astype(jnp.bfloat16)
    first = lax.broadcasted_iota(jnp.int32, (V_ROWS - HEAD_DIM, MOBA_BLOCK), 0) == 0
    ones_rows = jnp.where(first, 1.0, 0.0).astype(jnp.bfloat16)
    for j in range(PROJ_BLOCKS):
        vT_ref[0, j] = vT[:, j * MOBA_BLOCK:(j + 1) * MOBA_BLOCK]
        for h in range(N_HEADS):
            vT_ref[0, j, h * V_ROWS + HEAD_DIM:(h + 1) * V_ROWS, :] = ones_rows

    y = y_ref[...] + bdw_ref[...]
    mu = jnp.mean(y, axis=-1, keepdims=True)
    d = y - mu
    var = jnp.mean(d * d, axis=-1, keepdims=True)
    z = d * lax.rsqrt(var + EPS) * gln_ref[...] + bln_ref[...]
    conv_ref[0] = (z * _sigmoid(z)).astype(conv_ref.dtype)


def _in_proj(x, wq, wk, wv, wu, bglu, cosT, sinT, c, sa, sb, w_dw, b_dw, g_ln, b_ln):
    B, S, _ = x.shape
    nb = S // MOBA_BLOCK
    const = lambda shape: pl.BlockSpec(shape, lambda b, n: (0,) * len(shape))
    row = const((1, CONV_WIDTH))
    return pl.pallas_call(
        _in_proj_kernel,
        grid=(B, S // PROJ_TILE),
        in_specs=[
            pl.BlockSpec((1, PROJ_TILE, D_MODEL), lambda b, n: (b, n, 0)),
            const(wq.shape), const(wk.shape), const(wv.shape), const(wu.shape),
            const(bglu.shape),
            pl.BlockSpec((ROT_HALF, PROJ_TILE), lambda b, n: (0, n)),
            pl.BlockSpec((ROT_HALF, PROJ_TILE), lambda b, n: (0, n)),
            pl.BlockSpec((PROJ_TILE, K_LANES), lambda b, n: (n, 0)),
            pl.BlockSpec((PROJ_TILE, K_LANES), lambda b, n: (n, 0)),
            pl.BlockSpec((PROJ_TILE, K_LANES), lambda b, n: (n, 0)),
            const(w_dw.shape), row, row, row,
        ],
        out_specs=[
            pl.BlockSpec((1, N_HEADS * Q_ROWS, PROJ_TILE), lambda b, n: (b, 0, n)),
            pl.BlockSpec((1, PROJ_BLOCKS, MOBA_BLOCK, N_HEADS * K_LANES), lambda b, n: (b, n, 0, 0)),
            pl.BlockSpec((1, PROJ_BLOCKS, N_HEADS * V_ROWS, MOBA_BLOCK), lambda b, n: (b, n, 0, 0)),
            pl.BlockSpec((1, nb, N_HEADS * K_LANES), lambda b, n: (b, 0, 0)),
            pl.BlockSpec((1, PROJ_TILE, CONV_WIDTH), lambda b, n: (b, n, 0)),
        ],
        out_shape=[
            jax.ShapeDtypeStruct((B, N_HEADS * Q_ROWS, S), jnp.bfloat16),
            jax.ShapeDtypeStruct((B, nb, MOBA_BLOCK, N_HEADS * K_LANES), jnp.bfloat16),
            jax.ShapeDtypeStruct((B, nb, N_HEADS * V_ROWS, MOBA_BLOCK), jnp.bfloat16),
            jax.ShapeDtypeStruct((B, nb, N_HEADS * K_LANES), jnp.float32),
            jax.ShapeDtypeStruct((B, S, CONV_WIDTH), jnp.bfloat16),
        ],
        scratch_shapes=[
            pltpu.VMEM((SUBLANES, CONV_HALO + PROJ_TILE, CONV_WIDTH), jnp.float32),
            pltpu.VMEM((PROJ_TILE, CONV_WIDTH), jnp.float32),
        ],
        compiler_params=pltpu.CompilerParams(
            dimension_semantics=("parallel", "arbitrary"), vmem_limit_bytes=VMEM_LIMIT),
        name="in_proj",
    )(x, wq, wk, wv, wu, bglu, cosT, sinT, c, sa, sb, w_dw, b_dw, g_ln, b_ln)


HEADS_PER_STEP = 4


def _moba_kernel(qT_ref, k_ref, vT_ref, kmean_ref, o_ref, s_ref, p_ref):
    nb = k_ref.shape[1]
    kpos = lax.broadcasted_iota(jnp.int32, (MOBA_BLOCK, MOBA_BLOCK), 0)
    qpos = lax.broadcasted_iota(jnp.int32, (MOBA_BLOCK, MOBA_BLOCK), 1)
    causal = kpos <= qpos
    blk = lax.broadcasted_iota(jnp.int32, (nb, MOBA_BLOCK), 0)
    pad = jnp.zeros((Q_ROWS - HEAD_DIM - nb, MOBA_BLOCK), jnp.bfloat16)
    heads = range(HEADS_PER_STEP)

    def stage_values(h, n, p):
        vT_blk = vT_ref[0, n, h * V_ROWS:(h + 1) * V_ROWS, :]
        return jnp.dot(vT_blk, p, preferred_element_type=jnp.float32)

    def stage_softmax(s, m, own):
        if own:
            s = jnp.where(causal, s, -jnp.inf)
        m_new = jnp.maximum(m, jnp.max(s, axis=0, keepdims=True))
        return m_new, jnp.exp2(m - m_new), jnp.exp2(s - m_new).astype(jnp.bfloat16)

    def select(i):
        cols = pl.ds(pl.multiple_of(i * MOBA_BLOCK, MOBA_BLOCK), MOBA_BLOCK)
        qas = []
        for h in heads:
            qT = qT_ref[0, h * Q_ROWS:(h + 1) * Q_ROWS, cols]

            km = kmean_ref[0, :, h * K_LANES:(h + 1) * K_LANES]
            km_hi = km.astype(jnp.bfloat16)
            km_lo = (km - km_hi.astype(jnp.float32)).astype(jnp.bfloat16)
            g2 = jnp.dot(jnp.concatenate([km_hi, km_lo], axis=0), qT, preferred_element_type=jnp.float32)
            gate = g2[:nb] + g2[nb:]

            rem = jnp.where(blk < i, gate, -jnp.inf)
            sel = blk == i
            for _ in range(MOBA_TOP_K):
                mx = jnp.max(rem, axis=0, keepdims=True)
                first = jnp.min(jnp.where(rem == mx, blk, nb), axis=0, keepdims=True)
                pick = (blk == first) & (mx > -jnp.inf)
                sel = sel | pick
                rem = jnp.where(pick, -jnp.inf, rem)
            bias = jnp.where(sel, 0.0, MASKED).astype(jnp.bfloat16)

            qas.append(jnp.concatenate([qT[:HEAD_DIM], bias, pad], axis=0))
        return tuple(qas)

    def query_block(i, qas):
        cols = pl.ds(pl.multiple_of(i * MOBA_BLOCK, MOBA_BLOCK), MOBA_BLOCK)
        qas_next = select(jnp.minimum(i + 1, nb - 1))

        def stage_logits(h, n):
            k_blk = k_ref[0, n, :, h * K_LANES:(h + 1) * K_LANES]
            return jnp.dot(k_blk, qas[h], preferred_element_type=jnp.float32)

        for h in heads:
            s_ref[h] = stage_logits(h, 0)
            p_ref[h] = jnp.zeros((MOBA_BLOCK, MOBA_BLOCK), jnp.bfloat16)

        def values_behind(t, carry):
            pvs = [stage_values(h, jnp.maximum(t - 1, 0), p_ref[h]) for h in heads]
            return [acc * alpha + pv for (_, alpha, acc), pv in zip(carry, pvs)]

        def body(t, carry):
            ahead = [stage_logits(h, t + 1) for h in heads]
            accs = values_behind(t, carry)
            new = []
            for h in heads:
                m_new, alpha, p = stage_softmax(s_ref[h], carry[h][0], own=False)
                p_ref[h] = p
                new.append((m_new, alpha, accs[h]))
            for h in heads:
                s_ref[h] = ahead[h]
            return tuple(new)

        init = (jnp.full((1, MOBA_BLOCK), -jnp.inf, jnp.float32), jnp.ones((1, MOBA_BLOCK), jnp.float32),
                jnp.zeros((V_ROWS, MOBA_BLOCK), jnp.float32))
        carry = lax.fori_loop(0, i // 2, lambda tt, c: body(2 * tt + 1, body(2 * tt, c)),
                              (init,) * HEADS_PER_STEP)

        def own_block(carry):
            accs = values_behind(i, carry)
            for h in heads:
                _, alpha, p = stage_softmax(s_ref[h], carry[h][0], own=True)
                acc = accs[h] * alpha + stage_values(h, i, p)
                o_ref[0, h * HEAD_DIM:(h + 1) * HEAD_DIM, cols] = (
                    acc[:HEAD_DIM] * (1.0 / acc[HEAD_DIM:HEAD_DIM + 1])).astype(o_ref.dtype)

        lax.cond(i % 2 == 1, lambda c: own_block(body(i - 1, c)), own_block, carry)
        return qas_next

    lax.fori_loop(0, nb, query_block, select(0))


def _moba_attention(qT, k, vT, kmean):
    B, _, S = qT.shape
    nb = S // MOBA_BLOCK
    hp = HEADS_PER_STEP
    return pl.pallas_call(
        _moba_kernel,
        grid=(B, N_HEADS // hp),
        in_specs=[
            pl.BlockSpec((1, hp * Q_ROWS, S), lambda b, g: (b, g, 0)),
            pl.BlockSpec((1, nb, MOBA_BLOCK, hp * K_LANES), lambda b, g: (b, 0, 0, g)),
            pl.BlockSpec((1, nb, hp * V_ROWS, MOBA_BLOCK), lambda b, g: (b, 0, g, 0)),
            pl.BlockSpec((1, nb, hp * K_LANES), lambda b, g: (b, 0, g)),
        ],
        out_specs=pl.BlockSpec((1, hp * HEAD_DIM, S), lambda b, g: (b, g, 0)),
        out_shape=jax.ShapeDtypeStruct((B, ATTN_WIDTH, S), jnp.bfloat16),
        scratch_shapes=[
            pltpu.VMEM((hp, MOBA_BLOCK, MOBA_BLOCK), jnp.float32),
            pltpu.VMEM((hp, MOBA_BLOCK, MOBA_BLOCK), jnp.bfloat16),
        ],
        compiler_params=pltpu.CompilerParams(
            dimension_semantics=("parallel", "parallel"), vmem_limit_bytes=VMEM_LIMIT),
        name="moba_attn",
    )(qT, k, vT, kmean)


MLP_TILE = 1024
FF_CHUNK = 1024


def _out_mlp_kernel(x_ref, aT_ref, cv_ref, woa_ref, woc_ref, g_ref, w1_ref, w2_ref, gf_ref, o_ref,
                    h_ref, *, final_norm):
    attn = aT_ref[0].T
    h = x_ref[...]
    h = h + jnp.dot(attn, woa_ref[...], preferred_element_type=jnp.float32)
    h = h + jnp.dot(cv_ref[...], woc_ref[...], preferred_element_type=jnp.float32)
    h_ref[...] = h
    hn = _rms(h, g_ref[...]).astype(jnp.bfloat16)

    def chunk(c, carry):
        cols = pl.ds(pl.multiple_of(c * FF_CHUNK, FF_CHUNK), FF_CHUNK)
        ff = jnp.maximum(jnp.dot(hn, w1_ref[:, cols], preferred_element_type=jnp.float32), 0.0)
        h_ref[...] += jnp.dot((ff * ff).astype(jnp.bfloat16), w2_ref[c], preferred_element_type=jnp.float32)
        return carry

    lax.fori_loop(0, D_FF // FF_CHUNK, chunk, 0)
    h = h_ref[...]
    o_ref[...] = _rms(h, gf_ref[...]) if final_norm else h


def _out_mlp(x2d, attnT, conv2d, woa, woc, g_mlp, w1, w2, g_final, final_norm):
    T = x2d.shape[0]
    S = attnT.shape[2]
    per = S // MLP_TILE
    const = lambda shape: pl.BlockSpec(shape, lambda t: (0,) * len(shape), pipeline_mode=pl.Buffered(1))
    return pl.pallas_call(
        functools.partial(_out_mlp_kernel, final_norm=final_norm),
        grid=(T // MLP_TILE,),
        in_specs=[
            pl.BlockSpec((MLP_TILE, D_MODEL), lambda t: (t, 0)),
            pl.BlockSpec((1, ATTN_WIDTH, MLP_TILE), lambda t: (t // per, 0, t % per)),
            pl.BlockSpec((MLP_TILE, CONV_WIDTH), lambda t: (t, 0)),
            const(woa.shape), const(woc.shape), const((1, D_MODEL)),
            const(w1.shape), const(w2.shape), const((1, D_MODEL)),
        ],
        out_specs=pl.BlockSpec((MLP_TILE, D_MODEL), lambda t: (t, 0)),
        out_shape=jax.ShapeDtypeStruct((T, D_MODEL), jnp.float32),
        scratch_shapes=[pltpu.VMEM((MLP_TILE, D_MODEL), jnp.float32)],
        compiler_params=pltpu.CompilerParams(
            dimension_semantics=("parallel",), vmem_limit_bytes=VMEM_LIMIT),
        name="out_mlp",
    )(x2d, attnT, conv2d, woa, woc, g_mlp, w1, w2, g_final)


def _rope_tables(S):
    inv_freq = ROPE_THETA ** (-jnp.arange(ROT_HALF, dtype=jnp.float32) * 2.0 / ROT_DIM)
    ang = jnp.arange(S, dtype=jnp.float32)[:, None] * inv_freq[None, :]
    cos, sin = jnp.cos(ang), jnp.sin(ang)
    zeros = jnp.zeros((S, K_LANES - ROT_DIM), jnp.float32)
    zero8 = jnp.zeros((S, ROT_HALF), jnp.float32)
    c = jnp.concatenate([cos, cos, jnp.ones_like(zeros)], axis=1)
    sa = jnp.concatenate([-sin, zero8, zeros], axis=1)
    sb = jnp.concatenate([zero8, sin, zeros], axis=1)
    return cos.T, sin.T, c, sa, sb


def _split_w_in(w, g):
    bf = jnp.bfloat16
    w = w * g[:, None]
    wq = w[:, :ATTN_WIDTH].T.reshape(N_HEADS, HEAD_DIM, D_MODEL) * (HEAD_DIM ** -0.5 * LOG2E)
    wq = jnp.pad(wq, ((0, 0), (0, Q_ROWS - HEAD_DIM), (0, 0))).reshape(N_HEADS * Q_ROWS, D_MODEL)
    wk = w[:, ATTN_WIDTH:2 * ATTN_WIDTH].reshape(D_MODEL, N_HEADS, HEAD_DIM)
    wk = jnp.pad(wk, ((0, 0), (0, 0), (0, K_LANES - HEAD_DIM))).reshape(D_MODEL, N_HEADS * K_LANES)
    wv = w[:, 2 * ATTN_WIDTH:3 * ATTN_WIDTH].T.reshape(N_HEADS, HEAD_DIM, D_MODEL)
    wv = jnp.pad(wv, ((0, 0), (0, V_ROWS - HEAD_DIM), (0, 0))).reshape(N_HEADS * V_ROWS, D_MODEL)
    wu = w[:, 3 * ATTN_WIDTH:].reshape(D_MODEL, 2, CONV_WIDTH // LANES, LANES)
    wu = wu.transpose(0, 2, 1, 3).reshape(D_MODEL, 2 * CONV_WIDTH)
    return wq.astype(bf), wk.astype(bf), wv.astype(bf), wu.astype(bf)


def _split_b_glu(b):
    return b.reshape(2, CONV_WIDTH // LANES, 1, LANES).transpose(1, 2, 0, 3).reshape(CONV_WIDTH // LANES, 1, 2 * LANES)


def kernel(x, g_mix_norm, w_in, b_glu, w_dw, b_dw, g_conv_ln, b_conv_ln, w_out, g_mlp_norm,
           w_mlp_in, w_mlp_out, g_final):
    B, S, D = x.shape
    depth = w_in.shape[0]
    bf = jnp.bfloat16
    tables = _rope_tables(S)
    h = x
    for l in range(depth):
        wq, wk, wv, wu = _split_w_in(w_in[l], g_mix_norm[l])
        qT, k, vT, kmean, conv = _in_proj(
            h, wq, wk, wv, wu, _split_b_glu(b_glu[l]), *tables,
            w_dw[l][:, 0, :], b_dw[l][None], g_conv_ln[l][None], b_conv_ln[l][None])
        attnT = _moba_attention(qT, k, vT, kmean)
        last = l == depth - 1
        h = _out_mlp(
            h.reshape(B * S, D), attnT, conv.reshape(B * S, CONV_WIDTH),
            w_out[l][:ATTN_WIDTH].astype(bf), w_out[l][ATTN_WIDTH:].astype(bf), g_mlp_norm[l][None],
            w_mlp_in[l].astype(bf),
            w_mlp_out[l].reshape(D_FF // FF_CHUNK, FF_CHUNK, D).astype(bf), g_final[None], final_norm=last,
        ).reshape(B, S, D)
    return h
```

```python
import functools

import jax
import jax.numpy as jnp
from jax import lax
from jax.experimental import pallas as pl
from jax.experimental.pallas import tpu as pltpu

D_MODEL = 1024
ATTN_WIDTH = 512
CONV_WIDTH = 512
HEAD_DIM = 64
N_HEADS = 8
ROT_DIM = 16
ROT_HALF = 8
ROPE_THETA = 500000.0
MOBA_BLOCK = 256
MOBA_TOP_K = 3
CONV_K = 31
D_FF = 4096
EPS = 1e-6

Q_ROWS = 128
K_LANES = 128
BIAS_ROW0 = HEAD_DIM
V_ROWS = 80
MASKED = -1e30
LOG2E = 1.4426950408889634

VMEM_LIMIT = 56 * 1024 * 1024

_NT = (((1,), (1,)), ((), ()))


def _rms(x, g):
    return x * lax.rsqrt(jnp.mean(x * x, axis=-1, keepdims=True) + EPS) * g


def _sigmoid(x):
    return 1.0 / (1.0 + jnp.exp(-x))


CONV_HALO = 32
CONV_ROWS = 64
LANES = 128
SUBLANES = 8


PROJ_BLOCKS = 2
PROJ_TILE = PROJ_BLOCKS * MOBA_BLOCK


def _in_proj_kernel(x_ref, wq_ref, wk_ref, wv_ref, wu_ref, bglu_ref,
                    cosT_ref, sinT_ref, c_ref, sa_ref, sb_ref, wdw_ref, bdw_ref, gln_ref, bln_ref,
                    qT_ref, k_ref, vT_ref, kmean_ref, conv_ref, win_ref, y_ref):
    n = pl.program_id(1)

    @pl.when(n == 0)
    def _():
        win_ref[0, 0:CONV_HALO, :] = jnp.zeros((CONV_HALO, CONV_WIDTH), jnp.float32)

    x = x_ref[0]
    xb = (x * lax.rsqrt(jnp.mean(x * x, axis=-1, keepdims=True) + EPS)).astype(jnp.bfloat16)

    rows = CONV_HALO + PROJ_TILE
    off = CONV_HALO - (CONV_K - 1)
    for ch in range(CONV_WIDTH // LANES):
        lanes = slice(ch * LANES, (ch + 1) * LANES)
        u = jnp.dot(xb, wu_ref[:, 2 * ch * LANES:2 * (ch + 1) * LANES], preferred_element_type=jnp.float32) + bglu_ref[ch]
        win_ref[0, CONV_HALO:, lanes] = u[:, :LANES] * _sigmoid(u[:, LANES:])
        for b in range(1, SUBLANES):
            win_ref[b, 0:rows - SUBLANES, lanes] = win_ref[0, pl.ds(b, rows - SUBLANES), lanes]
        for r in range(PROJ_TILE // CONV_ROWS):
            acc = jnp.zeros((CONV_ROWS, LANES), jnp.float32)
            for j in range(CONV_K):
                a, b = divmod(off + j, SUBLANES)
                start = r * CONV_ROWS + a * SUBLANES
                acc = acc + win_ref[b, start:start + CONV_ROWS, lanes] * wdw_ref[j:j + 1, lanes]
            y_ref[r * CONV_ROWS:(r + 1) * CONV_ROWS, lanes] = acc
        win_ref[0, 0:CONV_HALO, lanes] = win_ref[0, PROJ_TILE:rows, lanes]

    qT = lax.dot_general(wq_ref[...], xb, _NT, preferred_element_type=jnp.float32)
    q3 = qT.reshape(N_HEADS, Q_ROWS, PROJ_TILE)
    cos = cosT_ref[...][None]
    sin = sinT_ref[...][None]
    x1 = q3[:, 0:ROT_HALF]
    x2 = q3[:, ROT_HALF:ROT_DIM]
    q3 = jnp.concatenate([x1 * cos - x2 * sin, x2 * cos + x1 * sin, q3[:, ROT_DIM:]], axis=1)
    qT_ref[0] = q3.reshape(N_HEADS * Q_ROWS, PROJ_TILE).astype(jnp.bfloat16)

    k = jnp.dot(xb, wk_ref[...], preferred_element_type=jnp.float32)
    c = c_ref[...]
    sa = sa_ref[...]
    sb = sb_ref[...]
    parts = []
    for h in range(N_HEADS):
        kh = k[:, h * K_LANES:(h + 1) * K_LANES]
        parts.append(kh * c + pltpu.roll(kh, K_LANES - ROT_HALF, 1) * sa + pltpu.roll(kh, ROT_HALF, 1) * sb)
    k = jnp.concatenate(parts, axis=1)
    lane = lax.broadcasted_iota(jnp.int32, (1, N_HEADS * K_LANES), 1) % K_LANES
    for j in range(PROJ_BLOCKS):
        blk = PROJ_BLOCKS * n + j
        kj = k[j * MOBA_BLOCK:(j + 1) * MOBA_BLOCK]
        kmean_ref[0, pl.ds(blk, 1), :] = jnp.mean(kj, axis=0, keepdims=True)
        k_ref[0, j] = (kj + jnp.where(lane == BIAS_ROW0 + blk, 1.0, 0.0)).astype(jnp.bfloat16)

    vT = lax.dot_general(wv_ref[...], xb, _NT, preferred_element_type=jnp.float32).astype(jnp.bfloat16)
    first = lax.broadcasted_iota(jnp.int32, (V_ROWS - HEAD_DIM, MOBA_BLOCK), 0) == 0
    ones_rows = jnp.where(first, 1.0, 0.0).astype(jnp.bfloat16)
    for j in range(PROJ_BLOCKS):
        vT_ref[0, j] = vT[:, j * MOBA_BLOCK:(j + 1) * MOBA_BLOCK]
        for h in range(N_HEADS):
            vT_ref[0, j, h * V_ROWS + HEAD_DIM:(h + 1) * V_ROWS, :] = ones_rows

    y = y_ref[...] + bdw_ref[...]
    mu = jnp.mean(y, axis=-1, keepdims=True)
    d = y - mu
    var = jnp.mean(d * d, axis=-1, keepdims=True)
    z = d * lax.rsqrt(var + EPS) * gln_ref[...] + bln_ref[...]
    conv_ref[0] = (z * _sigmoid(z)).astype(conv_ref.dtype)


def _in_proj(x, wq, wk, wv, wu, bglu, cosT, sinT, c, sa, sb, w_dw, b_dw, g_ln, b_ln):
    B, S, _ = x.shape
    nb = S // MOBA_BLOCK
    const = lambda shape: pl.BlockSpec(shape, lambda b, n: (0,) * len(shape))
    row = const((1, CONV_WIDTH))
    return pl.pallas_call(
        _in_proj_kernel,
        grid=(B, S // PROJ_TILE),
        in_specs=[
            pl.BlockSpec((1, PROJ_TILE, D_MODEL), lambda b, n: (b, n, 0)),
            const(wq.shape), const(wk.shape), const(wv.shape), const(wu.shape),
            const(bglu.shape),
            pl.BlockSpec((ROT_HALF, PROJ_TILE), lambda b, n: (0, n)),
            pl.BlockSpec((ROT_HALF, PROJ_TILE), lambda b, n: (0, n)),
            pl.BlockSpec((PROJ_TILE, K_LANES), lambda b, n: (n, 0)),
            pl.BlockSpec((PROJ_TILE, K_LANES), lambda b, n: (n, 0)),
            pl.BlockSpec((PROJ_TILE, K_LANES), lambda b, n: (n, 0)),
            const(w_dw.shape), row, row, row,
        ],
        out_specs=[
            pl.BlockSpec((1, N_HEADS * Q_ROWS, PROJ_TILE), lambda b, n: (b, 0, n)),
            pl.BlockSpec((1, PROJ_BLOCKS, MOBA_BLOCK, N_HEADS * K_LANES), lambda b, n: (b, n, 0, 0)),
            pl.BlockSpec((1, PROJ_BLOCKS, N_HEADS * V_ROWS, MOBA_BLOCK), lambda b, n: (b, n, 0, 0)),
            pl.BlockSpec((1, nb, N_HEADS * K_LANES), lambda b, n: (b, 0, 0)),
            pl.BlockSpec((1, PROJ_TILE, CONV_WIDTH), lambda b, n: (b, n, 0)),
        ],
        out_shape=[
            jax.ShapeDtypeStruct((B, N_HEADS * Q_ROWS, S), jnp.bfloat16),
            jax.ShapeDtypeStruct((B, nb, MOBA_BLOCK, N_HEADS * K_LANES), jnp.bfloat16),
            jax.ShapeDtypeStruct((B, nb, N_HEADS * V_ROWS, MOBA_BLOCK), jnp.bfloat16),
            jax.ShapeDtypeStruct((B, nb, N_HEADS * K_LANES), jnp.float32),
            jax.ShapeDtypeStruct((B, S, CONV_WIDTH), jnp.bfloat16),
        ],
        scratch_shapes=[
            pltpu.VMEM((SUBLANES, CONV_HALO + PROJ_TILE, CONV_WIDTH), jnp.float32),
            pltpu.VMEM((PROJ_TILE, CONV_WIDTH), jnp.float32),
        ],
        compiler_params=pltpu.CompilerParams(
            dimension_semantics=("parallel", "arbitrary"), vmem_limit_bytes=VMEM_LIMIT),
        name="in_proj",
    )(x, wq, wk, wv, wu, bglu, cosT, sinT, c, sa, sb, w_dw, b_dw, g_ln, b_ln)


HEADS_PER_STEP = 4


def _moba_kernel(qT_ref, k_ref, vT_ref, kmean_ref, o_ref, s_ref, p_ref, acc_ref):
    nb = k_ref.shape[1]
    kpos = lax.broadcasted_iota(jnp.int32, (MOBA_BLOCK, MOBA_BLOCK), 0)
    qpos = lax.broadcasted_iota(jnp.int32, (MOBA_BLOCK, MOBA_BLOCK), 1)
    causal = kpos <= qpos
    blk = lax.broadcasted_iota(jnp.int32, (nb, MOBA_BLOCK), 0)
    pad = jnp.zeros((Q_ROWS - HEAD_DIM - nb, MOBA_BLOCK), jnp.bfloat16)
    heads = range(HEADS_PER_STEP)

    def stage_values(h, n, p):
        vT_blk = vT_ref[0, n, h * V_ROWS:(h + 1) * V_ROWS, :]
        return jnp.dot(vT_blk, p, preferred_element_type=jnp.float32)

    def stage_softmax(s, m, own):
        if own:
            s = jnp.where(causal, s, -jnp.inf)
        m_new = jnp.maximum(m, jnp.max(s, axis=0, keepdims=True))
        return m_new, jnp.exp2(m - m_new), jnp.exp2(s - m_new).astype(jnp.bfloat16)

    def select(i):
        cols = pl.ds(pl.multiple_of(i * MOBA_BLOCK, MOBA_BLOCK), MOBA_BLOCK)
        qas = []
        for h in heads:
            qT = qT_ref[0, h * Q_ROWS:(h + 1) * Q_ROWS, cols]

            km = kmean_ref[0, :, h * K_LANES:(h + 1) * K_LANES]
            km_hi = km.astype(jnp.bfloat16)
            km_lo = (km - km_hi.astype(jnp.float32)).astype(jnp.bfloat16)
            g2 = jnp.dot(jnp.concatenate([km_hi, km_lo], axis=0), qT, preferred_element_type=jnp.float32)
            gate = g2[:nb] + g2[nb:]

            rem = jnp.where(blk < i, gate, -jnp.inf)
            sel = blk == i
            for _ in range(MOBA_TOP_K):
                mx = jnp.max(rem, axis=0, keepdims=True)
                first = jnp.min(jnp.where(rem == mx, blk, nb), axis=0, keepdims=True)
                pick = (blk == first) & (mx > -jnp.inf)
                sel = sel | pick
                rem = jnp.where(pick, -jnp.inf, rem)
            bias = jnp.where(sel, 0.0, MASKED).astype(jnp.bfloat16)

            qas.append(jnp.concatenate([qT[:HEAD_DIM], bias, pad], axis=0))
        return tuple(qas)

    def query_block(i, qas):
        cols = pl.ds(pl.multiple_of(i * MOBA_BLOCK, MOBA_BLOCK), MOBA_BLOCK)
        qas_next = select(jnp.minimum(i + 1, nb - 1))

        def stage_logits(h, n):
            k_blk = k_ref[0, n, :, h * K_LANES:(h + 1) * K_LANES]
            return jnp.dot(k_blk, qas[h], preferred_element_type=jnp.float32)

        for h in heads:
            s_ref[h] = stage_logits(h, 0)
            p_ref[h] = jnp.zeros((MOBA_BLOCK, MOBA_BLOCK), jnp.bfloat16)
            acc_ref[h] = jnp.zeros((V_ROWS, MOBA_BLOCK), jnp.float32)

        def values_behind(t, carry):
            pvs = [stage_values(h, jnp.maximum(t - 1, 0), p_ref[h]) for h in heads]
            for h in heads:
                acc_ref[h] = acc_ref[h] * carry[h][1] + pvs[h]

        def body(t, carry):
            ahead = [stage_logits(h, t + 1) for h in heads]
            values_behind(t, carry)
            new = []
            for h in heads:
                m_new, alpha, p = stage_softmax(s_ref[h], carry[h][0], own=False)
                p_ref[h] = p
                new.append((m_new, alpha))
            for h in heads:
                s_ref[h] = ahead[h]
            return tuple(new)

        init = (jnp.full((1, MOBA_BLOCK), -jnp.inf, jnp.float32), jnp.ones((1, MOBA_BLOCK), jnp.float32))
        carry = lax.fori_loop(0, i // 2, lambda tt, c: body(2 * tt + 1, body(2 * tt, c)),
                              (init,) * HEADS_PER_STEP)

        def own_block(carry):
            values_behind(i, carry)
            for h in heads:
                _, alpha, p = stage_softmax(s_ref[h], carry[h][0], own=True)
                acc = acc_ref[h] * alpha + stage_values(h, i, p)
                o_ref[0, h * HEAD_DIM:(h + 1) * HEAD_DIM, cols] = (
                    acc[:HEAD_DIM] * (1.0 / acc[HEAD_DIM:HEAD_DIM + 1])).astype(o_ref.dtype)

        lax.cond(i % 2 == 1, lambda c: own_block(body(i - 1, c)), own_block, carry)
        return qas_next

    lax.fori_loop(0, nb, query_block, select(0))


def _moba_attention(qT, k, vT, kmean):
    B, _, S = qT.shape
    nb = S // MOBA_BLOCK
    hp = HEADS_PER_STEP
    return pl.pallas_call(
        _moba_kernel,
        grid=(B, N_HEADS // hp),
        in_specs=[
            pl.BlockSpec((1, hp * Q_ROWS, S), lambda b, g: (b, g, 0)),
            pl.BlockSpec((1, nb, MOBA_BLOCK, hp * K_LANES), lambda b, g: (b, 0, 0, g)),
            pl.BlockSpec((1, nb, hp * V_ROWS, MOBA_BLOCK), lambda b, g: (b, 0, g, 0)),
            pl.BlockSpec((1, nb, hp * K_LANES), lambda b, g: (b, 0, g)),
        ],
        out_specs=pl.BlockSpec((1, hp * HEAD_DIM, S), lambda b, g: (b, g, 0)),
        out_shape=jax.ShapeDtypeStruct((B, ATTN_WIDTH, S), jnp.bfloat16),
        scratch_shapes=[
            pltpu.VMEM((hp, MOBA_BLOCK, MOBA_BLOCK), jnp.float32),
            pltpu.VMEM((hp, MOBA_BLOCK, MOBA_BLOCK), jnp.bfloat16),
            pltpu.VMEM((hp, V_ROWS, MOBA_BLOCK), jnp.float32),
        ],
        compiler_params=pltpu.CompilerParams(
            dimension_semantics=("parallel", "parallel"), vmem_limit_bytes=VMEM_LIMIT),
        name="moba_attn",
    )(qT, k, vT, kmean)


MLP_TILE = 1024
FF_CHUNK = 1024


def _out_mlp_kernel(x_ref, aT_ref, cv_ref, woa_ref, woc_ref, g_ref, w1_ref, w2_ref, gf_ref, o_ref,
                    h_ref, *, final_norm):
    attn = aT_ref[0].T
    h = x_ref[...]
    h = h + jnp.dot(attn, woa_ref[...], preferred_element_type=jnp.float32)
    h = h + jnp.dot(cv_ref[...], woc_ref[...], preferred_element_type=jnp.float32)
    h_ref[...] = h
    hn = _rms(h, g_ref[...]).astype(jnp.bfloat16)

    def chunk(c, carry):
        cols = pl.ds(pl.multiple_of(c * FF_CHUNK, FF_CHUNK), FF_CHUNK)
        ff = jnp.maximum(jnp.dot(hn, w1_ref[:, cols], preferred_element_type=jnp.float32), 0.0)
        h_ref[...] += jnp.dot((ff * ff).astype(jnp.bfloat16), w2_ref[c], preferred_element_type=jnp.float32)
        return carry

    lax.fori_loop(0, D_FF // FF_CHUNK, chunk, 0)
    h = h_ref[...]
    o_ref[...] = _rms(h, gf_ref[...]) if final_norm else h


def _out_mlp(x2d, attnT, conv2d, woa, woc, g_mlp, w1, w2, g_final, final_norm):
    T = x2d.shape[0]
    S = attnT.shape[2]
    per = S // MLP_TILE
    const = lambda shape: pl.BlockSpec(shape, lambda t: (0,) * len(shape), pipeline_mode=pl.Buffered(1))
    return pl.pallas_call(
        functools.partial(_out_mlp_kernel, final_norm=final_norm),
        grid=(T // MLP_TILE,),
        in_specs=[
            pl.BlockSpec((MLP_TILE, D_MODEL), lambda t: (t, 0)),
            pl.BlockSpec((1, ATTN_WIDTH, MLP_TILE), lambda t: (t // per, 0, t % per)),
            pl.BlockSpec((MLP_TILE, CONV_WIDTH), lambda t: (t, 0)),
            const(woa.shape), const(woc.shape), const((1, D_MODEL)),
            const(w1.shape), const(w2.shape), const((1, D_MODEL)),
        ],
        out_specs=pl.BlockSpec((MLP_TILE, D_MODEL), lambda t: (t, 0)),
        out_shape=jax.ShapeDtypeStruct((T, D_MODEL), jnp.float32),
        scratch_shapes=[pltpu.VMEM((MLP_TILE, D_MODEL), jnp.float32)],
        compiler_params=pltpu.CompilerParams(
            dimension_semantics=("parallel",), vmem_limit_bytes=VMEM_LIMIT),
        name="out_mlp",
    )(x2d, attnT, conv2d, woa, woc, g_mlp, w1, w2, g_final)


def _rope_tables(S):
    inv_freq = ROPE_THETA ** (-jnp.arange(ROT_HALF, dtype=jnp.float32) * 2.0 / ROT_DIM)
    ang = jnp.arange(S, dtype=jnp.float32)[:, None] * inv_freq[None, :]
    cos, sin = jnp.cos(ang), jnp.sin(ang)
    zeros = jnp.zeros((S, K_LANES - ROT_DIM), jnp.float32)
    zero8 = jnp.zeros((S, ROT_HALF), jnp.float32)
    c = jnp.concatenate([cos, cos, jnp.ones_like(zeros)], axis=1)
    sa = jnp.concatenate([-sin, zero8, zeros], axis=1)
    sb = jnp.concatenate([zero8, sin, zeros], axis=1)
    return cos.T, sin.T, c, sa, sb


def _split_w_in(w, g):
    bf = jnp.bfloat16
    w = w * g[:, None]
    wq = w[:, :ATTN_WIDTH].T.reshape(N_HEADS, HEAD_DIM, D_MODEL) * (HEAD_DIM ** -0.5 * LOG2E)
    wq = jnp.pad(wq, ((0, 0), (0, Q_ROWS - HEAD_DIM), (0, 0))).reshape(N_HEADS * Q_ROWS, D_MODEL)
    wk = w[:, ATTN_WIDTH:2 * ATTN_WIDTH].reshape(D_MODEL, N_HEADS, HEAD_DIM)
    wk = jnp.pad(wk, ((0, 0), (0, 0), (0, K_LANES - HEAD_DIM))).reshape(D_MODEL, N_HEADS * K_LANES)
    wv = w[:, 2 * ATTN_WIDTH:3 * ATTN_WIDTH].T.reshape(N_HEADS, HEAD_DIM, D_MODEL)
    wv = jnp.pad(wv, ((0, 0), (0, V_ROWS - HEAD_DIM), (0, 0))).reshape(N_HEADS * V_ROWS, D_MODEL)
    wu = w[:, 3 * ATTN_WIDTH:].reshape(D_MODEL, 2, CONV_WIDTH // LANES, LANES)
    wu = wu.transpose(0, 2, 1, 3).reshape(D_MODEL, 2 * CONV_WIDTH)
    return wq.astype(bf), wk.astype(bf), wv.astype(bf), wu.astype(bf)


def _split_b_glu(b):
    return b.reshape(2, CONV_WIDTH // LANES, 1, LANES).transpose(1, 2, 0, 3).reshape(CONV_WIDTH // LANES, 1, 2 * LANES)


def kernel(x, g_mix_norm, w_in, b_glu, w_dw, b_dw, g_conv_ln, b_conv_ln, w_out, g_mlp_norm,
           w_mlp_in, w_mlp_out, g_final):
    B, S, D = x.shape
    depth = w_in.shape[0]
    bf = jnp.bfloat16
    tables = _rope_tables(S)
    h = x
    for l in range(depth):
        wq, wk, wv, wu = _split_w_in(w_in[l], g_mix_norm[l])
        qT, k, vT, kmean, conv = _in_proj(
            h, wq, wk, wv, wu, _split_b_glu(b_glu[l]), *tables,
            w_dw[l][:, 0, :], b_dw[l][None], g_conv_ln[l][None], b_conv_ln[l][None])
        attnT = _moba_attention(qT, k, vT, kmean)
        last = l == depth - 1
        h = _out_mlp(
            h.reshape(B * S, D), attnT, conv.reshape(B * S, CONV_WIDTH),
            w_out[l][:ATTN_WIDTH].astype(bf), w_out[l][ATTN_WIDTH:].astype(bf), g_mlp_norm[l][None],
            w_mlp_in[l].astype(bf),
            w_mlp_out[l].reshape(D_FF // FF_CHUNK, FF_CHUNK, D).astype(bf), g_final[None], final_norm=last,
        ).reshape(B, S, D)
    return h
```

```python
import functools

import jax
import jax.numpy as jnp
from jax import lax
from jax.experimental import pallas as pl
from jax.experimental.pallas import tpu as pltpu

D_MODEL = 1024
ATTN_WIDTH = 512
CONV_WIDTH = 512
HEAD_DIM = 64
N_HEADS = 8
ROT_DIM = 16
ROT_HALF = 8
ROPE_THETA = 500000.0
MOBA_BLOCK = 256
MOBA_TOP_K = 3
CONV_K = 31
D_FF = 4096
EPS = 1e-6

Q_ROWS = 128
K_LANES = 128
BIAS_ROW0 = HEAD_DIM
V_ROWS = 80
MASKED = -1e30
LOG2E = 1.4426950408889634

VMEM_LIMIT = 56 * 1024 * 1024

_NT = (((1,), (1,)), ((), ()))


def _rms(x, g):
    return x * lax.rsqrt(jnp.mean(x * x, axis=-1, keepdims=True) + EPS) * g


def _sigmoid(x):
    return 1.0 / (1.0 + jnp.exp(-x))


CONV_HALO = 32
CONV_ROWS = 64
LANES = 128
SUBLANES = 8


PROJ_BLOCKS = 2
PROJ_TILE = PROJ_BLOCKS * MOBA_BLOCK


def _in_proj_kernel(x_ref, wq_ref, wk_ref, wv_ref, wu_ref, bglu_ref,
                    cosT_ref, sinT_ref, c_ref, sa_ref, sb_ref, wdw_ref, bdw_ref, gln_ref, bln_ref,
                    qT_ref, k_ref, vT_ref, kmean_ref, conv_ref, win_ref, y_ref):
    n = pl.program_id(1)

    @pl.when(n == 0)
    def _():
        win_ref[0, 0:CONV_HALO, :] = jnp.zeros((CONV_HALO, CONV_WIDTH), jnp.float32)

    x = x_ref[0]
    xb = (x * lax.rsqrt(jnp.mean(x * x, axis=-1, keepdims=True) + EPS)).astype(jnp.bfloat16)

    rows = CONV_HALO + PROJ_TILE
    off = CONV_HALO - (CONV_K - 1)
    for ch in range(CONV_WIDTH // LANES):
        lanes = slice(ch * LANES, (ch + 1) * LANES)
        u = jnp.dot(xb, wu_ref[:, 2 * ch * LANES:2 * (ch + 1) * LANES], preferred_element_type=jnp.float32) + bglu_ref[ch]
        win_ref[0, CONV_HALO:, lanes] = u[:, :LANES] * _sigmoid(u[:, LANES:])
        for b in range(1, SUBLANES):
            win_ref[b, 0:rows - SUBLANES, lanes] = win_ref[0, pl.ds(b, rows - SUBLANES), lanes]
        for r in range(PROJ_TILE // CONV_ROWS):
            acc = jnp.broadcast_to(bdw_ref[:, lanes], (CONV_ROWS, LANES))
            for j in range(CONV_K):
                a, b = divmod(off + j, SUBLANES)
                start = r * CONV_ROWS + a * SUBLANES
                acc = acc + win_ref[b, start:start + CONV_ROWS, lanes] * wdw_ref[j:j + 1, lanes]
            y_ref[r * CONV_ROWS:(r + 1) * CONV_ROWS, lanes] = acc
        win_ref[0, 0:CONV_HALO, lanes] = win_ref[0, PROJ_TILE:rows, lanes]

    qT = lax.dot_general(wq_ref[...], xb, _NT, preferred_element_type=jnp.float32)
    q3 = qT.reshape(N_HEADS, Q_ROWS, PROJ_TILE)
    cos = cosT_ref[...][None]
    sin = sinT_ref[...][None]
    x1 = q3[:, 0:ROT_HALF]
    x2 = q3[:, ROT_HALF:ROT_DIM]
    q3 = jnp.concatenate([x1 * cos - x2 * sin, x2 * cos + x1 * sin, q3[:, ROT_DIM:]], axis=1)
    qT_ref[0] = q3.reshape(N_HEADS * Q_ROWS, PROJ_TILE).astype(jnp.bfloat16)

    k = jnp.dot(xb, wk_ref[...], preferred_element_type=jnp.float32)
    c = c_ref[...]
    sa = sa_ref[...]
    sb = sb_ref[...]
    parts = []
    for h in range(N_HEADS):
        kh = k[:, h * K_LANES:(h + 1) * K_LANES]
        parts.append(kh * c + pltpu.roll(kh, K_LANES - ROT_HALF, 1) * sa + pltpu.roll(kh, ROT_HALF, 1) * sb)
    k = jnp.concatenate(parts, axis=1)
    lane = lax.broadcasted_iota(jnp.int32, (1, N_HEADS * K_LANES), 1) % K_LANES
    for j in range(PROJ_BLOCKS):
        blk = PROJ_BLOCKS * n + j
        kj = k[j * MOBA_BLOCK:(j + 1) * MOBA_BLOCK]
        kmean_ref[0, pl.ds(blk, 1), :] = jnp.mean(kj, axis=0, keepdims=True)
        k_ref[0, j] = (kj + jnp.where(lane == BIAS_ROW0 + blk, 1.0, 0.0)).astype(jnp.bfloat16)

    vT = lax.dot_general(wv_ref[...], xb, _NT, preferred_element_type=jnp.float32).astype(jnp.bfloat16)
    first = lax.broadcasted_iota(jnp.int32, (V_ROWS - HEAD_DIM, MOBA_BLOCK), 0) == 0
    ones_rows = jnp.where(first, 1.0, 0.0).astype(jnp.bfloat16)
    for j in range(PROJ_BLOCKS):
        vT_ref[0, j] = vT[:, j * MOBA_BLOCK:(j + 1) * MOBA_BLOCK]
        for h in range(N_HEADS):
            vT_ref[0, j, h * V_ROWS + HEAD_DIM:(h + 1) * V_ROWS, :] = ones_rows

    y = y_ref[...]
    mu = jnp.mean(y, axis=-1, keepdims=True)
    d = y - mu
    var = jnp.mean(d * d, axis=-1, keepdims=True)
    z = d * lax.rsqrt(var + EPS) * gln_ref[...] + bln_ref[...]
    conv_ref[0] = (z * _sigmoid(z)).astype(conv_ref.dtype)


def _in_proj(x, wq, wk, wv, wu, bglu, cosT, sinT, c, sa, sb, w_dw, b_dw, g_ln, b_ln):
    B, S, _ = x.shape
    nb = S // MOBA_BLOCK
    const = lambda shape: pl.BlockSpec(shape, lambda b, n: (0,) * len(shape))
    row = const((1, CONV_WIDTH))
    return pl.pallas_call(
        _in_proj_kernel,
        grid=(B, S // PROJ_TILE),
        in_specs=[
            pl.BlockSpec((1, PROJ_TILE, D_MODEL), lambda b, n: (b, n, 0)),
            const(wq.shape), const(wk.shape), const(wv.shape), const(wu.shape),
            const(bglu.shape),
            pl.BlockSpec((ROT_HALF, PROJ_TILE), lambda b, n: (0, n)),
            pl.BlockSpec((ROT_HALF, PROJ_TILE), lambda b, n: (0, n)),
            pl.BlockSpec((PROJ_TILE, K_LANES), lambda b, n: (n, 0)),
            pl.BlockSpec((PROJ_TILE, K_LANES), lambda b, n: (n, 0)),
            pl.BlockSpec((PROJ_TILE, K_LANES), lambda b, n: (n, 0)),
            const(w_dw.shape), row, row, row,
        ],
        out_specs=[
            pl.BlockSpec((1, N_HEADS * Q_ROWS, PROJ_TILE), lambda b, n: (b, 0, n)),
            pl.BlockSpec((1, PROJ_BLOCKS, MOBA_BLOCK, N_HEADS * K_LANES), lambda b, n: (b, n, 0, 0)),
            pl.BlockSpec((1, PROJ_BLOCKS, N_HEADS * V_ROWS, MOBA_BLOCK), lambda b, n: (b, n, 0, 0)),
            pl.BlockSpec((1, nb, N_HEADS * K_LANES), lambda b, n: (b, 0, 0)),
            pl.BlockSpec((1, PROJ_TILE, CONV_WIDTH), lambda b, n: (b, n, 0)),
        ],
        out_shape=[
            jax.ShapeDtypeStruct((B, N_HEADS * Q_ROWS, S), jnp.bfloat16),
            jax.ShapeDtypeStruct((B, nb, MOBA_BLOCK, N_HEADS * K_LANES), jnp.bfloat16),
            jax.ShapeDtypeStruct((B, nb, N_HEADS * V_ROWS, MOBA_BLOCK), jnp.bfloat16),
            jax.ShapeDtypeStruct((B, nb, N_HEADS * K_LANES), jnp.float32),
            jax.ShapeDtypeStruct((B, S, CONV_WIDTH), jnp.bfloat16),
        ],
        scratch_shapes=[
            pltpu.VMEM((SUBLANES, CONV_HALO + PROJ_TILE, CONV_WIDTH), jnp.float32),
            pltpu.VMEM((PROJ_TILE, CONV_WIDTH), jnp.float32),
        ],
        compiler_params=pltpu.CompilerParams(
            dimension_semantics=("parallel", "arbitrary"), vmem_limit_bytes=VMEM_LIMIT),
        name="in_proj",
    )(x, wq, wk, wv, wu, bglu, cosT, sinT, c, sa, sb, w_dw, b_dw, g_ln, b_ln)


HEADS_PER_STEP = 4


def _moba_kernel(qT_ref, k_ref, vT_ref, kmean_ref, o_ref, s_ref, p_ref, acc_ref):
    nb = k_ref.shape[1]
    kpos = lax.broadcasted_iota(jnp.int32, (MOBA_BLOCK, MOBA_BLOCK), 0)
    qpos = lax.broadcasted_iota(jnp.int32, (MOBA_BLOCK, MOBA_BLOCK), 1)
    causal = kpos <= qpos
    blk = lax.broadcasted_iota(jnp.int32, (nb, MOBA_BLOCK), 0)
    pad = jnp.zeros((Q_ROWS - HEAD_DIM - nb, MOBA_BLOCK), jnp.bfloat16)
    heads = range(HEADS_PER_STEP)

    def stage_values(h, n, p):
        vT_blk = vT_ref[0, n, h * V_ROWS:(h + 1) * V_ROWS, :]
        return jnp.dot(vT_blk, p, preferred_element_type=jnp.float32)

    def stage_softmax(s, m, own):
        if own:
            s = jnp.where(causal, s, -jnp.inf)
        m_new = jnp.maximum(m, jnp.max(s, axis=0, keepdims=True))
        return m_new, jnp.exp2(m - m_new), jnp.exp2(s - m_new).astype(jnp.bfloat16)

    def select(i):
        cols = pl.ds(pl.multiple_of(i * MOBA_BLOCK, MOBA_BLOCK), MOBA_BLOCK)
        qas = []
        for h in heads:
            qT = qT_ref[0, h * Q_ROWS:(h + 1) * Q_ROWS, cols]

            km = kmean_ref[0, :, h * K_LANES:(h + 1) * K_LANES]
            km_hi = km.astype(jnp.bfloat16)
            km_lo = (km - km_hi.astype(jnp.float32)).astype(jnp.bfloat16)
            g2 = jnp.dot(jnp.concatenate([km_hi, km_lo], axis=0), qT, preferred_element_type=jnp.float32)
            gate = g2[:nb] + g2[nb:]

            rem = jnp.where(blk < i, gate, -jnp.inf)
            sel = blk == i
            for _ in range(MOBA_TOP_K):
                mx = jnp.max(rem, axis=0, keepdims=True)
                first = jnp.min(jnp.where(rem == mx, blk, nb), axis=0, keepdims=True)
                pick = (blk == first) & (mx > -jnp.inf)
                sel = sel | pick
                rem = jnp.where(pick, -jnp.inf, rem)
            bias = jnp.where(sel, 0.0, MASKED).astype(jnp.bfloat16)

            qas.append(jnp.concatenate([qT[:HEAD_DIM], bias, pad], axis=0))
        return tuple(qas)

    def query_block(i, qas):
        cols = pl.ds(pl.multiple_of(i * MOBA_BLOCK, MOBA_BLOCK), MOBA_BLOCK)
        qas_next = select(jnp.minimum(i + 1, nb - 1))

        def stage_logits(h, n):
            k_blk = k_ref[0, n, :, h * K_LANES:(h + 1) * K_LANES]
            return jnp.dot(k_blk, qas[h], preferred_element_type=jnp.float32)

        for h in heads:
            s_ref[h] = stage_logits(h, 0)
            p_ref[h] = jnp.zeros((MOBA_BLOCK, MOBA_BLOCK), jnp.bfloat16)
            acc_ref[h] = jnp.zeros((V_ROWS, MOBA_BLOCK), jnp.float32)

        def values_behind(t, carry):
            pvs = [stage_values(h, jnp.maximum(t - 1, 0), p_ref[h]) for h in heads]
            for h in heads:
                acc_ref[h] = acc_ref[h] * carry[h][1] + pvs[h]

        def body(t, carry):
            ahead = [stage_logits(h, t + 1) for h in heads]
            values_behind(t, carry)
            new = []
            for h in heads:
                m_new, alpha, p = stage_softmax(s_ref[h], carry[h][0], own=False)
                p_ref[h] = p
                new.append((m_new, alpha))
            for h in heads:
                s_ref[h] = ahead[h]
            return tuple(new)

        init = (jnp.full((1, MOBA_BLOCK), -jnp.inf, jnp.float32), jnp.ones((1, MOBA_BLOCK), jnp.float32))
        carry = lax.fori_loop(0, i // 2, lambda tt, c: body(2 * tt + 1, body(2 * tt, c)),
                              (init,) * HEADS_PER_STEP)

        def own_block(carry):
            values_behind(i, carry)
            for h in heads:
                _, alpha, p = stage_softmax(s_ref[h], carry[h][0], own=True)
                acc = acc_ref[h] * alpha + stage_values(h, i, p)
                o_ref[0, h * HEAD_DIM:(h + 1) * HEAD_DIM, cols] = (
                    acc[:HEAD_DIM] * (1.0 / acc[HEAD_DIM:HEAD_DIM + 1])).astype(o_ref.dtype)

        lax.cond(i % 2 == 1, lambda c: own_block(body(i - 1, c)), own_block, carry)
        return qas_next

    lax.fori_loop(0, nb, query_block, select(0))


def _moba_attention(qT, k, vT, kmean):
    B, _, S = qT.shape
    nb = S // MOBA_BLOCK
    hp = HEADS_PER_STEP
    return pl.pallas_call(
        _moba_kernel,
        grid=(B, N_HEADS // hp),
        in_specs=[
            pl.BlockSpec((1, hp * Q_ROWS, S), lambda b, g: (b, g, 0)),
            pl.BlockSpec((1, nb, MOBA_BLOCK, hp * K_LANES), lambda b, g: (b, 0, 0, g)),
            pl.BlockSpec((1, nb, hp * V_ROWS, MOBA_BLOCK), lambda b, g: (b, 0, g, 0)),
            pl.BlockSpec((1, nb, hp * K_LANES), lambda b, g: (b, 0, g)),
        ],
        out_specs=pl.BlockSpec((1, hp * HEAD_DIM, S), lambda b, g: (b, g, 0)),
        out_shape=jax.ShapeDtypeStruct((B, ATTN_WIDTH, S), jnp.bfloat16),
        scratch_shapes=[
            pltpu.VMEM((hp, MOBA_BLOCK, MOBA_BLOCK), jnp.float32),
            pltpu.VMEM((hp, MOBA_BLOCK, MOBA_BLOCK), jnp.bfloat16),
            pltpu.VMEM((hp, V_ROWS, MOBA_BLOCK), jnp.float32),
        ],
        compiler_params=pltpu.CompilerParams(
            dimension_semantics=("parallel", "parallel"), vmem_limit_bytes=VMEM_LIMIT),
        name="moba_attn",
    )(qT, k, vT, kmean)


MLP_TILE = 1024
FF_CHUNK = 2048


def _out_mlp_kernel(x_ref, aT_ref, cv_ref, woa_ref, woc_ref, g_ref, w1_ref, w2_ref, gf_ref, o_ref,
                    h_ref, *, final_norm):
    attn = aT_ref[0].T
    h = x_ref[...]
    h = h + jnp.dot(attn, woa_ref[...], preferred_element_type=jnp.float32)
    h = h + jnp.dot(cv_ref[...], woc_ref[...], preferred_element_type=jnp.float32)
    h_ref[...] = h
    hn = _rms(h, g_ref[...]).astype(jnp.bfloat16)

    def chunk(c, carry):
        cols = pl.ds(pl.multiple_of(c * FF_CHUNK, FF_CHUNK), FF_CHUNK)
        ff = jnp.maximum(jnp.dot(hn, w1_ref[:, cols], preferred_element_type=jnp.float32), 0.0)
        h_ref[...] += jnp.dot((ff * ff).astype(jnp.bfloat16), w2_ref[c], preferred_element_type=jnp.float32)
        return carry

    lax.fori_loop(0, D_FF // FF_CHUNK, chunk, 0)
    h = h_ref[...]
    o_ref[...] = _rms(h, gf_ref[...]) if final_norm else h


def _out_mlp(x2d, attnT, conv2d, woa, woc, g_mlp, w1, w2, g_final, final_norm):
    T = x2d.shape[0]
    S = attnT.shape[2]
    per = S // MLP_TILE
    const = lambda shape: pl.BlockSpec(shape, lambda t: (0,) * len(shape), pipeline_mode=pl.Buffered(1))
    return pl.pallas_call(
        functools.partial(_out_mlp_kernel, final_norm=final_norm),
        grid=(T // MLP_TILE,),
        in_specs=[
            pl.BlockSpec((MLP_TILE, D_MODEL), lambda t: (t, 0)),
            pl.BlockSpec((1, ATTN_WIDTH, MLP_TILE), lambda t: (t // per, 0, t % per)),
            pl.BlockSpec((MLP_TILE, CONV_WIDTH), lambda t: (t, 0)),
            const(woa.shape), const(woc.shape), const((1, D_MODEL)),
            const(w1.shape), const(w2.shape), const((1, D_MODEL)),
        ],
        out_specs=pl.BlockSpec((MLP_TILE, D_MODEL), lambda t: (t, 0)),
        out_shape=jax.ShapeDtypeStruct((T, D_MODEL), jnp.float32),
        scratch_shapes=[pltpu.VMEM((MLP_TILE, D_MODEL), jnp.float32)],
        compiler_params=pltpu.CompilerParams(
            dimension_semantics=("parallel",), vmem_limit_bytes=VMEM_LIMIT),
        name="out_mlp",
    )(x2d, attnT, conv2d, woa, woc, g_mlp, w1, w2, g_final)


def _rope_tables(S):
    inv_freq = ROPE_THETA ** (-jnp.arange(ROT_HALF, dtype=jnp.float32) * 2.0 / ROT_DIM)
    ang = jnp.arange(S, dtype=jnp.float32)[:, None] * inv_freq[None, :]
    cos, sin = jnp.cos(ang), jnp.sin(ang)
    zeros = jnp.zeros((S, K_LANES - ROT_DIM), jnp.float32)
    zero8 = jnp.zeros((S, ROT_HALF), jnp.float32)
    c = jnp.concatenate([cos, cos, jnp.ones_like(zeros)], axis=1)
    sa = jnp.concatenate([-sin, zero8, zeros], axis=1)
    sb = jnp.concatenate([zero8, sin, zeros], axis=1)
    return cos.T, sin.T, c, sa, sb


def _split_w_in(w, g):
    bf = jnp.bfloat16
    w = w * g[:, None]
    wq = w[:, :ATTN_WIDTH].T.reshape(N_HEADS, HEAD_DIM, D_MODEL) * (HEAD_DIM ** -0.5 * LOG2E)
    wq = jnp.pad(wq, ((0, 0), (0, Q_ROWS - HEAD_DIM), (0, 0))).reshape(N_HEADS * Q_ROWS, D_MODEL)
    wk = w[:, ATTN_WIDTH:2 * ATTN_WIDTH].reshape(D_MODEL, N_HEADS, HEAD_DIM)
    wk = jnp.pad(wk, ((0, 0), (0, 0), (0, K_LANES - HEAD_DIM))).reshape(D_MODEL, N_HEADS * K_LANES)
    wv = w[:, 2 * ATTN_WIDTH:3 * ATTN_WIDTH].T.reshape(N_HEADS, HEAD_DIM, D_MODEL)
    wv = jnp.pad(wv, ((0, 0), (0, V_ROWS - HEAD_DIM), (0, 0))).reshape(N_HEADS * V_ROWS, D_MODEL)
    wu = w[:, 3 * ATTN_WIDTH:].reshape(D_MODEL, 2, CONV_WIDTH // LANES, LANES)
    wu = wu.transpose(0, 2, 1, 3).reshape(D_MODEL, 2 * CONV_WIDTH)
    return wq.astype(bf), wk.astype(bf), wv.astype(bf), wu.astype(bf)


def _split_b_glu(b):
    return b.reshape(2, CONV_WIDTH // LANES, 1, LANES).transpose(1, 2, 0, 3).reshape(CONV_WIDTH // LANES, 1, 2 * LANES)


def kernel(x, g_mix_norm, w_in, b_glu, w_dw, b_dw, g_conv_ln, b_conv_ln, w_out, g_mlp_norm,
           w_mlp_in, w_mlp_out, g_final):
    B, S, D = x.shape
    depth = w_in.shape[0]
    bf = jnp.bfloat16
    tables = _rope_tables(S)
    h = x
    for l in range(depth):
        wq, wk, wv, wu = _split_w_in(w_in[l], g_mix_norm[l])
        qT, k, vT, kmean, conv = _in_proj(
            h, wq, wk, wv, wu, _split_b_glu(b_glu[l]), *tables,
            w_dw[l][:, 0, :], b_dw[l][None], g_conv_ln[l][None], b_conv_ln[l][None])
        attnT = _moba_attention(qT, k, vT, kmean)
        last = l == depth - 1
        h = _out_mlp(
            h.reshape(B * S, D), attnT, conv.reshape(B * S, CONV_WIDTH),
            w_out[l][:ATTN_WIDTH].astype(bf), w_out[l][ATTN_WIDTH:].astype(bf), g_mlp_norm[l][None],
            w_mlp_in[l].astype(bf),
            w_mlp_out[l].reshape(D_FF // FF_CHUNK, FF_CHUNK, D).astype(bf), g_final[None], final_norm=last,
        ).reshape(B, S, D)
    return h
```

```python
import functools

import jax
import jax.numpy as jnp
from jax import lax
from jax.experimental import pallas as pl
from jax.experimental.pallas import tpu as pltpu

D_MODEL = 1024
ATTN_WIDTH = 512
CONV_WIDTH = 512
HEAD_DIM = 64
N_HEADS = 8
ROT_DIM = 16
ROT_HALF = 8
ROPE_THETA = 500000.0
MOBA_BLOCK = 256
MOBA_TOP_K = 3
CONV_K = 31
D_FF = 4096
EPS = 1e-6

Q_ROWS = 128
K_LANES = 128
BIAS_ROW0 = HEAD_DIM
V_ROWS = 80
MASKED = -1e30
LOG2E = 1.4426950408889634

VMEM_LIMIT = 56 * 1024 * 1024

_NT = (((1,), (1,)), ((), ()))


def _rms(x, g):
    return x * lax.rsqrt(jnp.mean(x * x, axis=-1, keepdims=True) + EPS) * g


def _sigmoid(x):
    return 1.0 / (1.0 + jnp.exp(-x))


CONV_HALO = 32
CONV_ROWS = 64
LANES = 128
SUBLANES = 8


PROJ_BLOCKS = 2
PROJ_TILE = PROJ_BLOCKS * MOBA_BLOCK


def _in_proj_kernel(x_ref, wq_ref, wk_ref, wv_ref, wu_ref, bglu_ref,
                    cosT_ref, sinT_ref, c_ref, sa_ref, sb_ref, wdw_ref, bdw_ref, gln_ref, bln_ref,
                    qT_ref, k_ref, vT_ref, kmean_ref, conv_ref, win_ref, y_ref):
    n = pl.program_id(1)

    @pl.when(n == 0)
    def _():
        win_ref[0, 0:CONV_HALO, :] = jnp.zeros((CONV_HALO, CONV_WIDTH), jnp.float32)

    x = x_ref[0]
    xb = (x * lax.rsqrt(jnp.mean(x * x, axis=-1, keepdims=True) + EPS)).astype(jnp.bfloat16)

    rows = CONV_HALO + PROJ_TILE
    off = CONV_HALO - (CONV_K - 1)
    for ch in range(CONV_WIDTH // LANES):
        lanes = slice(ch * LANES, (ch + 1) * LANES)
        u = jnp.dot(xb, wu_ref[:, 2 * ch * LANES:2 * (ch + 1) * LANES], preferred_element_type=jnp.float32) + bglu_ref[ch]
        win_ref[0, CONV_HALO:, lanes] = u[:, :LANES] * _sigmoid(u[:, LANES:])
        for b in range(1, SUBLANES):
            win_ref[b, 0:rows - SUBLANES, lanes] = win_ref[0, pl.ds(b, rows - SUBLANES), lanes]
        for r in range(PROJ_TILE // CONV_ROWS):
            acc = jnp.broadcast_to(bdw_ref[:, lanes], (CONV_ROWS, LANES))
            for j in range(CONV_K):
                a, b = divmod(off + j, SUBLANES)
                start = r * CONV_ROWS + a * SUBLANES
                acc = acc + win_ref[b, start:start + CONV_ROWS, lanes] * wdw_ref[j:j + 1, lanes]
            y_ref[r * CONV_ROWS:(r + 1) * CONV_ROWS, lanes] = acc
        win_ref[0, 0:CONV_HALO, lanes] = win_ref[0, PROJ_TILE:rows, lanes]

    qT = lax.dot_general(wq_ref[...], xb, _NT, preferred_element_type=jnp.float32)
    q3 = qT.reshape(N_HEADS, Q_ROWS, PROJ_TILE)
    cos = cosT_ref[...][None]
    sin = sinT_ref[...][None]
    x1 = q3[:, 0:ROT_HALF]
    x2 = q3[:, ROT_HALF:ROT_DIM]
    q3 = jnp.concatenate([x1 * cos - x2 * sin, x2 * cos + x1 * sin, q3[:, ROT_DIM:]], axis=1)
    qT_ref[0] = q3.reshape(N_HEADS * Q_ROWS, PROJ_TILE).astype(jnp.bfloat16)

    k = jnp.dot(xb, wk_ref[...], preferred_element_type=jnp.float32)
    c = c_ref[...]
    sa = sa_ref[...]
    sb = sb_ref[...]
    parts = []
    for h in range(N_HEADS):
        kh = k[:, h * K_LANES:(h + 1) * K_LANES]
        parts.append(kh * c + pltpu.roll(kh, K_LANES - ROT_HALF, 1) * sa + pltpu.roll(kh, ROT_HALF, 1) * sb)
    k = jnp.concatenate(parts, axis=1)
    lane = lax.broadcasted_iota(jnp.int32, (1, N_HEADS * K_LANES), 1) % K_LANES
    for j in range(PROJ_BLOCKS):
        blk = PROJ_BLOCKS * n + j
        kj = k[j * MOBA_BLOCK:(j + 1) * MOBA_BLOCK]
        kmean_ref[0, pl.ds(blk, 1), :] = jnp.mean(kj, axis=0, keepdims=True)
        k_ref[0, j] = (kj + jnp.where(lane == BIAS_ROW0 + blk, 1.0, 0.0)).astype(jnp.bfloat16)

    vT = lax.dot_general(wv_ref[...], xb, _NT, preferred_element_type=jnp.float32).astype(jnp.bfloat16)
    first = lax.broadcasted_iota(jnp.int32, (V_ROWS - HEAD_DIM, MOBA_BLOCK), 0) == 0
    ones_rows = jnp.where(first, 1.0, 0.0).astype(jnp.bfloat16)
    for j in range(PROJ_BLOCKS):
        vT_ref[0, j] = vT[:, j * MOBA_BLOCK:(j + 1) * MOBA_BLOCK]
        for h in range(N_HEADS):
            vT_ref[0, j, h * V_ROWS + HEAD_DIM:(h + 1) * V_ROWS, :] = ones_rows

    y = y_ref[...]
    mu = jnp.mean(y, axis=-1, keepdims=True)
    d = y - mu
    var = jnp.mean(d * d, axis=-1, keepdims=True)
    z = d * lax.rsqrt(var + EPS) * gln_ref[...] + bln_ref[...]
    conv_ref[0] = (z * _sigmoid(z)).astype(conv_ref.dtype)


def _in_proj(x, wq, wk, wv, wu, bglu, cosT, sinT, c, sa, sb, w_dw, b_dw, g_ln, b_ln):
    B, S, _ = x.shape
    nb = S // MOBA_BLOCK
    const = lambda shape: pl.BlockSpec(shape, lambda b, n: (0,) * len(shape))
    row = const((1, CONV_WIDTH))
    return pl.pallas_call(
        _in_proj_kernel,
        grid=(B, S // PROJ_TILE),
        in_specs=[
            pl.BlockSpec((1, PROJ_TILE, D_MODEL), lambda b, n: (b, n, 0)),
            const(wq.shape), const(wk.shape), const(wv.shape), const(wu.shape),
            const(bglu.shape),
            pl.BlockSpec((ROT_HALF, PROJ_TILE), lambda b, n: (0, n)),
            pl.BlockSpec((ROT_HALF, PROJ_TILE), lambda b, n: (0, n)),
            pl.BlockSpec((PROJ_TILE, K_LANES), lambda b, n: (n, 0)),
            pl.BlockSpec((PROJ_TILE, K_LANES), lambda b, n: (n, 0)),
            pl.BlockSpec((PROJ_TILE, K_LANES), lambda b, n: (n, 0)),
            const(w_dw.shape), row, row, row,
        ],
        out_specs=[
            pl.BlockSpec((1, N_HEADS * Q_ROWS, PROJ_TILE), lambda b, n: (b, 0, n)),
            pl.BlockSpec((1, PROJ_BLOCKS, MOBA_BLOCK, N_HEADS * K_LANES), lambda b, n: (b, n, 0, 0)),
            pl.BlockSpec((1, PROJ_BLOCKS, N_HEADS * V_ROWS, MOBA_BLOCK), lambda b, n: (b, n, 0, 0)),
            pl.BlockSpec((1, nb, N_HEADS * K_LANES), lambda b, n: (b, 0, 0)),
            pl.BlockSpec((1, PROJ_TILE, CONV_WIDTH), lambda b, n: (b, n, 0)),
        ],
        out_shape=[
            jax.ShapeDtypeStruct((B, N_HEADS * Q_ROWS, S), jnp.bfloat16),
            jax.ShapeDtypeStruct((B, nb, MOBA_BLOCK, N_HEADS * K_LANES), jnp.bfloat16),
            jax.ShapeDtypeStruct((B, nb, N_HEADS * V_ROWS, MOBA_BLOCK), jnp.bfloat16),
            jax.ShapeDtypeStruct((B, nb, N_HEADS * K_LANES), jnp.float32),
            jax.ShapeDtypeStruct((B, S, CONV_WIDTH), jnp.bfloat16),
        ],
        scratch_shapes=[
            pltpu.VMEM((SUBLANES, CONV_HALO + PROJ_TILE, CONV_WIDTH), jnp.float32),
            pltpu.VMEM((PROJ_TILE, CONV_WIDTH), jnp.float32),
        ],
        compiler_params=pltpu.CompilerParams(
            dimension_semantics=("parallel", "arbitrary"), vmem_limit_bytes=VMEM_LIMIT),
        name="in_proj",
    )(x, wq, wk, wv, wu, bglu, cosT, sinT, c, sa, sb, w_dw, b_dw, g_ln, b_ln)


HEADS_PER_STEP = 4


def _moba_kernel(qT_ref, k_ref, vT_ref, kmean_ref, o_ref, s_ref, p_ref, acc_ref):
    nb = k_ref.shape[1]
    kpos = lax.broadcasted_iota(jnp.int32, (MOBA_BLOCK, MOBA_BLOCK), 0)
    qpos = lax.broadcasted_iota(jnp.int32, (MOBA_BLOCK, MOBA_BLOCK), 1)
    causal = kpos <= qpos
    blk = lax.broadcasted_iota(jnp.int32, (nb, MOBA_BLOCK), 0)
    pad = jnp.zeros((Q_ROWS - HEAD_DIM - nb, MOBA_BLOCK), jnp.bfloat16)
    heads = range(HEADS_PER_STEP)

    def stage_values(h, n, p):
        vT_blk = vT_ref[0, n, h * V_ROWS:(h + 1) * V_ROWS, :]
        return jnp.dot(vT_blk, p, preferred_element_type=jnp.float32)

    def stage_softmax(s, m, own):
        if own:
            s = jnp.where(causal, s, -jnp.inf)
        m_new, alpha, p = [], [], []
        for c in range(MOBA_BLOCK // LANES):
            lanes = slice(c * LANES, (c + 1) * LANES)
            sc, mc = s[:, lanes], m[:, lanes]
            mn = jnp.maximum(mc, jnp.max(sc, axis=0, keepdims=True))
            m_new.append(mn)
            alpha.append(jnp.exp2(mc - mn))
            p.append(jnp.exp2(sc - mn).astype(jnp.bfloat16))
        return jnp.concatenate(m_new, axis=1), jnp.concatenate(alpha, axis=1), jnp.concatenate(p, axis=1)

    def select(i):
        cols = pl.ds(pl.multiple_of(i * MOBA_BLOCK, MOBA_BLOCK), MOBA_BLOCK)
        qas = []
        for h in heads:
            qT = qT_ref[0, h * Q_ROWS:(h + 1) * Q_ROWS, cols]

            km = kmean_ref[0, :, h * K_LANES:(h + 1) * K_LANES]
            km_hi = km.astype(jnp.bfloat16)
            km_lo = (km - km_hi.astype(jnp.float32)).astype(jnp.bfloat16)
            g2 = jnp.dot(jnp.concatenate([km_hi, km_lo], axis=0), qT, preferred_element_type=jnp.float32)
            gate = g2[:nb] + g2[nb:]

            rem = jnp.where(blk < i, gate, -jnp.inf)
            sel = blk == i
            for _ in range(MOBA_TOP_K):
                mx = jnp.max(rem, axis=0, keepdims=True)
                first = jnp.min(jnp.where(rem == mx, blk, nb), axis=0, keepdims=True)
                pick = (blk == first) & (mx > -jnp.inf)
                sel = sel | pick
                rem = jnp.where(pick, -jnp.inf, rem)
            bias = jnp.where(sel, 0.0, MASKED).astype(jnp.bfloat16)

            qas.append(jnp.concatenate([qT[:HEAD_DIM], bias, pad], axis=0))
        return tuple(qas)

    def query_block(i, qas):
        cols = pl.ds(pl.multiple_of(i * MOBA_BLOCK, MOBA_BLOCK), MOBA_BLOCK)
        qas_next = select(jnp.minimum(i + 1, nb - 1))

        def stage_logits(h, n):
            k_blk = k_ref[0, n, :, h * K_LANES:(h + 1) * K_LANES]
            return jnp.dot(k_blk, qas[h], preferred_element_type=jnp.float32)

        for h in heads:
            s_ref[h] = stage_logits(h, 0)
            p_ref[h] = jnp.zeros((MOBA_BLOCK, MOBA_BLOCK), jnp.bfloat16)
            acc_ref[h] = jnp.zeros((V_ROWS, MOBA_BLOCK), jnp.float32)

        def values_behind(t, carry):
            pvs = [stage_values(h, jnp.maximum(t - 1, 0), p_ref[h]) for h in heads]
            for h in heads:
                acc_ref[h] = acc_ref[h] * carry[h][1] + pvs[h]

        def body(t, carry):
            ahead = [stage_logits(h, t + 1) for h in heads]
            values_behind(t, carry)
            new = []
            for h in heads:
                m_new, alpha, p = stage_softmax(s_ref[h], carry[h][0], own=False)
                p_ref[h] = p
                new.append((m_new, alpha))
            for h in heads:
                s_ref[h] = ahead[h]
            return tuple(new)

        init = (jnp.full((1, MOBA_BLOCK), -jnp.inf, jnp.float32), jnp.ones((1, MOBA_BLOCK), jnp.float32))
        carry = lax.fori_loop(0, i // 2, lambda tt, c: body(2 * tt + 1, body(2 * tt, c)),
                              (init,) * HEADS_PER_STEP)

        def own_block(carry):
            values_behind(i, carry)
            for h in heads:
                _, alpha, p = stage_softmax(s_ref[h], carry[h][0], own=True)
                acc = acc_ref[h] * alpha + stage_values(h, i, p)
                o_ref[0, h * HEAD_DIM:(h + 1) * HEAD_DIM, cols] = (
                    acc[:HEAD_DIM] * (1.0 / acc[HEAD_DIM:HEAD_DIM + 1])).astype(o_ref.dtype)

        lax.cond(i % 2 == 1, lambda c: own_block(body(i - 1, c)), own_block, carry)
        return qas_next

    lax.fori_loop(0, nb, query_block, select(0))


def _moba_attention(qT, k, vT, kmean):
    B, _, S = qT.shape
    nb = S // MOBA_BLOCK
    hp = HEADS_PER_STEP
    return pl.pallas_call(
        _moba_kernel,
        grid=(B, N_HEADS // hp),
        in_specs=[
            pl.BlockSpec((1, hp * Q_ROWS, S), lambda b, g: (b, g, 0)),
            pl.BlockSpec((1, nb, MOBA_BLOCK, hp * K_LANES), lambda b, g: (b, 0, 0, g)),
            pl.BlockSpec((1, nb, hp * V_ROWS, MOBA_BLOCK), lambda b, g: (b, 0, g, 0)),
            pl.BlockSpec((1, nb, hp * K_LANES), lambda b, g: (b, 0, g)),
        ],
        out_specs=pl.BlockSpec((1, hp * HEAD_DIM, S), lambda b, g: (b, g, 0)),
        out_shape=jax.ShapeDtypeStruct((B, ATTN_WIDTH, S), jnp.bfloat16),
        scratch_shapes=[
            pltpu.VMEM((hp, MOBA_BLOCK, MOBA_BLOCK), jnp.float32),
            pltpu.VMEM((hp, MOBA_BLOCK, MOBA_BLOCK), jnp.bfloat16),
            pltpu.VMEM((hp, V_ROWS, MOBA_BLOCK), jnp.float32),
        ],
        compiler_params=pltpu.CompilerParams(
            dimension_semantics=("parallel", "parallel"), vmem_limit_bytes=VMEM_LIMIT),
        name="moba_attn",
    )(qT, k, vT, kmean)


MLP_TILE = 1024
FF_CHUNK = 2048


def _out_mlp_kernel(x_ref, aT_ref, cv_ref, woa_ref, woc_ref, g_ref, w1_ref, w2_ref, gf_ref, o_ref,
                    h_ref, *, final_norm):
    attn = aT_ref[0].T
    h = x_ref[...]
    h = h + jnp.dot(attn, woa_ref[...], preferred_element_type=jnp.float32)
    h = h + jnp.dot(cv_ref[...], woc_ref[...], preferred_element_type=jnp.float32)
    h_ref[...] = h
    hn = _rms(h, g_ref[...]).astype(jnp.bfloat16)

    def chunk(c, carry):
        cols = pl.ds(pl.multiple_of(c * FF_CHUNK, FF_CHUNK), FF_CHUNK)
        ff = jnp.maximum(jnp.dot(hn, w1_ref[:, cols], preferred_element_type=jnp.float32), 0.0)
        h_ref[...] += jnp.dot((ff * ff).astype(jnp.bfloat16), w2_ref[c], preferred_element_type=jnp.float32)
        return carry

    lax.fori_loop(0, D_FF // FF_CHUNK, chunk, 0)
    h = h_ref[...]
    o_ref[...] = _rms(h, gf_ref[...]) if final_norm else h


def _out_mlp(x2d, attnT, conv2d, woa, woc, g_mlp, w1, w2, g_final, final_norm):
    T = x2d.shape[0]
    S = attnT.shape[2]
    per = S // MLP_TILE
    const = lambda shape: pl.BlockSpec(shape, lambda t: (0,) * len(shape), pipeline_mode=pl.Buffered(1))
    return pl.pallas_call(
        functools.partial(_out_mlp_kernel, final_norm=final_norm),
        grid=(T // MLP_TILE,),
        in_specs=[
            pl.BlockSpec((MLP_TILE, D_MODEL), lambda t: (t, 0)),
            pl.BlockSpec((1, ATTN_WIDTH, MLP_TILE), lambda t: (t // per, 0, t % per)),
            pl.BlockSpec((MLP_TILE, CONV_WIDTH), lambda t: (t, 0)),
            const(woa.shape), const(woc.shape), const((1, D_MODEL)),
            const(w1.shape), const(w2.shape), const((1, D_MODEL)),
        ],
        out_specs=pl.BlockSpec((MLP_TILE, D_MODEL), lambda t: (t, 0)),
        out_shape=jax.ShapeDtypeStruct((T, D_MODEL), jnp.float32),
        scratch_shapes=[pltpu.VMEM((MLP_TILE, D_MODEL), jnp.float32)],
        compiler_params=pltpu.CompilerParams(
            dimension_semantics=("parallel",), vmem_limit_bytes=VMEM_LIMIT),
        name="out_mlp",
    )(x2d, attnT, conv2d, woa, woc, g_mlp, w1, w2, g_final)


def _rope_tables(S):
    inv_freq = ROPE_THETA ** (-jnp.arange(ROT_HALF, dtype=jnp.float32) * 2.0 / ROT_DIM)
    ang = jnp.arange(S, dtype=jnp.float32)[:, None] * inv_freq[None, :]
    cos, sin = jnp.cos(ang), jnp.sin(ang)
    zeros = jnp.zeros((S, K_LANES - ROT_DIM), jnp.float32)
    zero8 = jnp.zeros((S, ROT_HALF), jnp.float32)
    c = jnp.concatenate([cos, cos, jnp.ones_like(zeros)], axis=1)
    sa = jnp.concatenate([-sin, zero8, zeros], axis=1)
    sb = jnp.concatenate([zero8, sin, zeros], axis=1)
    return cos.T, sin.T, c, sa, sb


def _split_w_in(w, g):
    bf = jnp.bfloat16
    w = w * g[:, None]
    wq = w[:, :ATTN_WIDTH].T.reshape(N_HEADS, HEAD_DIM, D_MODEL) * (HEAD_DIM ** -0.5 * LOG2E)
    wq = jnp.pad(wq, ((0, 0), (0, Q_ROWS - HEAD_DIM), (0, 0))).reshape(N_HEADS * Q_ROWS, D_MODEL)
    wk = w[:, ATTN_WIDTH:2 * ATTN_WIDTH].reshape(D_MODEL, N_HEADS, HEAD_DIM)
    wk = jnp.pad(wk, ((0, 0), (0, 0), (0, K_LANES - HEAD_DIM))).reshape(D_MODEL, N_HEADS * K_LANES)
    wv = w[:, 2 * ATTN_WIDTH:3 * ATTN_WIDTH].T.reshape(N_HEADS, HEAD_DIM, D_MODEL)
    wv = jnp.pad(wv, ((0, 0), (0, V_ROWS - HEAD_DIM), (0, 0))).reshape(N_HEADS * V_ROWS, D_MODEL)
    wu = w[:, 3 * ATTN_WIDTH:].reshape(D_MODEL, 2, CONV_WIDTH // LANES, LANES)
    wu = wu.transpose(0, 2, 1, 3).reshape(D_MODEL, 2 * CONV_WIDTH)
    return wq.astype(bf), wk.astype(bf), wv.astype(bf), wu.astype(bf)


def _split_b_glu(b):
    return b.reshape(2, CONV_WIDTH // LANES, 1, LANES).transpose(1, 2, 0, 3).reshape(CONV_WIDTH // LANES, 1, 2 * LANES)


def kernel(x, g_mix_norm, w_in, b_glu, w_dw, b_dw, g_conv_ln, b_conv_ln, w_out, g_mlp_norm,
           w_mlp_in, w_mlp_out, g_final):
    B, S, D = x.shape
    depth = w_in.shape[0]
    bf = jnp.bfloat16
    tables = _rope_tables(S)
    h = x
    for l in range(depth):
        wq, wk, wv, wu = _split_w_in(w_in[l], g_mix_norm[l])
        qT, k, vT, kmean, conv = _in_proj(
            h, wq, wk, wv, wu, _split_b_glu(b_glu[l]), *tables,
            w_dw[l][:, 0, :], b_dw[l][None], g_conv_ln[l][None], b_conv_ln[l][None])
        attnT = _moba_attention(qT, k, vT, kmean)
        last = l == depth - 1
        h = _out_mlp(
            h.reshape(B * S, D), attnT, conv.reshape(B * S, CONV_WIDTH),
            w_out[l][:ATTN_WIDTH].astype(bf), w_out[l][ATTN_WIDTH:].astype(bf), g_mlp_norm[l][None],
            w_mlp_in[l].astype(bf),
            w_mlp_out[l].reshape(D_FF // FF_CHUNK, FF_CHUNK, D).astype(bf), g_final[None], final_norm=last,
        ).reshape(B, S, D)
    return h
```
